```python
import math
import jax, jax.numpy as jnp
from jax import lax
import numpy as np

D_MODEL = 1024
BATCH = 16
SEQ = 256
DEPTH = 2
DEC_BATCH = 2
DEC_SEQ = 4096
PAST_LEN = 512

GRID_W = 64
D_SSD = D_MODEL
HEADDIM = 64
N_HEADS = D_SSD // HEADDIM
N_BC_GROUPS = 2
D_STATE = 128
D_CONV = 4
CHUNK = 128
CONV_DIM = D_SSD + 2 * N_BC_GROUPS * D_STATE
D_POOL = D_MODEL
POOL_WINDOWS = (2, 4, 8, 16)
N_POOL_GROUPS = len(POOL_WINDOWS)
POOL_GROUP_W = D_POOL // N_POOL_GROUPS
D_MIX = D_SSD + D_POOL
D_IN_PROJ = D_SSD + CONV_DIM + 2 * N_HEADS + D_POOL
N_EXPERTS = 16
N_EXPERT_GROUPS = 4
EXPERTS_PER_GROUP = N_EXPERTS // N_EXPERT_GROUPS
TOP_K = 2
D_FF_EXPERT = D_MODEL // 2
EPS = 1e-6

kernel_name = "hybrid_ssd_pool_moe_diffusion_step"

F32 = jnp.float32


def rmsnorm(x, w):
    xf = x.astype(F32)
    y = xf * lax.rsqrt(jnp.mean(xf * xf, axis=-1, keepdims=True) + EPS)
    return (y * w.astype(F32)).astype(x.dtype)


def centred_depthwise_conv(x, w, b):
    pad_lo = (D_CONV - 1) // 2
    pad_hi = D_CONV - 1 - pad_lo
    y = lax.conv_general_dilated(x, w.astype(x.dtype)[:, None, :], window_strides=(1,),
                                 padding=[(pad_lo, pad_hi)],
                                 dimension_numbers=("NWC", "WIO", "NWC"),
                                 feature_group_count=x.shape[-1])
    return y + b.astype(x.dtype)


def segsum(x):
    T = x.shape[-1]
    xr = jnp.broadcast_to(x[..., None], x.shape + (T,))
    xr = jnp.where(jnp.tril(jnp.ones((T, T), bool), -1), xr, 0.0)
    cs = jnp.cumsum(xr, axis=-2)
    return jnp.where(jnp.tril(jnp.ones((T, T), bool), 0), cs, -jnp.inf)


def ssd_chunked(x, dt, a, Bh, Ch, init):
    b, L, h, p = x.shape
    n = Bh.shape[-1]
    nc = L // CHUNK
    x = x.astype(F32)
    xd = (x * dt[..., None]).reshape(b, nc, CHUNK, h, p)
    A = jnp.transpose((dt * a).reshape(b, nc, CHUNK, h), (0, 3, 1, 2))
    Bc = Bh.astype(F32).reshape(b, nc, CHUNK, h, n)
    Cc = Ch.astype(F32).reshape(b, nc, CHUNK, h, n)
    A_cum = jnp.cumsum(A, axis=-1)
    scores = jnp.einsum("bclhn,bcshn->bhcls", Cc, Bc) * jnp.exp(segsum(A))
    y_diag = jnp.einsum("bhcls,bcshp->bclhp", scores, xd)
    decay_states = jnp.exp(A_cum[..., -1:] - A_cum)
    states = jnp.einsum("bclhn,bhcl,bclhp->bchpn", Bc, decay_states, xd)
    states = jnp.concatenate([init.astype(F32)[:, None], states], axis=1)
    chunk_A = jnp.pad(A_cum[..., -1], ((0, 0), (0, 0), (1, 0)))
    new_states = jnp.einsum("bhzc,bchpn->bzhpn", jnp.exp(segsum(chunk_A)), states)
    states, final = new_states[:, :-1], new_states[:, -1]
    y_off = jnp.einsum("bclhn,bchpn,bhcl->bclhp", Cc, states, jnp.exp(A_cum))
    return (y_diag + y_off).reshape(b, L, h, p), final


def ssd_direction(xs, dt_raw, Bh, Ch, dt_bias, a_log, d_skip, init, reverse):
    if reverse:
        xs, dt_raw, Bh, Ch = (jnp.flip(t, axis=1) for t in (xs, dt_raw, Bh, Ch))
    dt = jax.nn.softplus(dt_raw.astype(F32) + dt_bias.astype(F32))
    y, final = ssd_chunked(xs, dt, -jnp.exp(a_log.astype(F32)), Bh, Ch, init)
    y = y + d_skip.astype(F32)[:, None] * xs.astype(F32)
    if reverse:
        y = jnp.flip(y, axis=1)
    return y, final


def window_bounds(n, w):
    idx = np.arange(n)
    lo = np.clip(idx - w // 2, 0, n - 1)
    hi = np.clip(idx + w // 2 - 1, 0, n - 1)
    return lo, hi


def pool_mean_1d(v, w):
    L = v.shape[1]
    cs = jnp.pad(jnp.cumsum(v, axis=1), ((0, 0), (1, 0), (0, 0)))
    lo, hi = window_bounds(L, w)
    s = jnp.take(cs, hi + 1, axis=1) - jnp.take(cs, lo, axis=1)
    cnt = (hi - lo + 1).astype(np.float32)
    return s / cnt[None, :, None]


def pool_mean_2d(v, w, rows):
    b, L, C = v.shape
    v4 = v.reshape(b, rows, GRID_W, C)
    S = jnp.pad(jnp.cumsum(jnp.cumsum(v4, axis=1), axis=2), ((0, 0), (1, 0), (1, 0), (0, 0)))
    r0, r1 = window_bounds(rows, w)
    c0, c1 = window_bounds(GRID_W, w)
    Dr = jnp.take(S, r1 + 1, axis=1) - jnp.take(S, r0, axis=1)
    s = jnp.take(Dr, c1 + 1, axis=2) - jnp.take(Dr, c0, axis=2)
    cnt = ((r1 - r0 + 1)[:, None] * (c1 - c0 + 1)[None, :]).astype(np.float32)
    return (s / cnt[None, :, :, None]).reshape(b, L, C)


def pool_mixer(v, pool_w, pool_scale, grid):
    b, L, _ = v.shape
    vf = v.astype(F32)
    outs = []
    for g, w in enumerate(POOL_WINDOWS):
        vg = vf[..., g * POOL_GROUP_W:(g + 1) * POOL_GROUP_W]
        m = pool_mean_2d(vg, w, L // GRID_W) if grid else pool_mean_1d(vg, w)
        outs.append(m - vg)
    u = jnp.stack(outs, axis=2)
    u = jnp.einsum("blgc,gcd->blgd", u, pool_w.astype(F32)).reshape(b, L, D_POOL)
    return u * pool_scale.astype(F32)


def routed_moe(h, w_router, router_bias, w_gate, w_up, w_down):
    b, L, d = h.shape
    t = h.reshape(b * L, d)
    probs = jax.nn.softmax(t.astype(F32) @ w_router.astype(F32), axis=-1)
    sel = probs + router_bias.astype(F32)
    grp_score = lax.top_k(sel.reshape(-1, N_EXPERT_GROUPS, EXPERTS_PER_GROUP), TOP_K)[0].sum(-1)
    best = jnp.argmax(grp_score, axis=-1)
    in_group = (jnp.arange(N_EXPERTS) // EXPERTS_PER_GROUP)[None, :] == best[:, None]
    _, idx = lax.top_k(jnp.where(in_group, sel, -jnp.inf), TOP_K)
    gate = jnp.take_along_axis(probs, idx, axis=-1)
    gate = gate / jnp.sum(gate, axis=-1, keepdims=True)
    combine = jnp.sum(jax.nn.one_hot(idx, N_EXPERTS, dtype=F32) * gate[..., None], axis=1)
    g = jnp.einsum("td,edf->tef", t, w_gate)
    u = jnp.einsum("td,edf->tef", t, w_up)
    a = jax.nn.silu(g) * u * combine[..., None].astype(g.dtype)
    out = jnp.einsum("tef,efd->td", a, w_down)
    return out.reshape(b, L, d)


def trunk_layer(x, cvec, init_f, init_b, grid, w_ada, b_ada, norm1_w, w_in, conv_w, conv_b,
                dt_bias, a_log, d_skip, ssd_norm_w, pool_w, pool_scale, w_out, norm2_w,
                w_router, router_bias, w_gate, w_up, w_down):
    b, L, _ = x.shape
    ada = jax.nn.silu(cvec.astype(F32)) @ w_ada.astype(F32) + b_ada.astype(F32)
    shift1, scale1, gate1, shift2, scale2, gate2 = jnp.split(ada[:, None, :], 6, axis=-1)
    h = rmsnorm(x, norm1_w) * (1.0 + scale1) + shift1
    proj = h @ w_in
    z, xBC, dt_raw, v = jnp.split(proj, [D_SSD, D_SSD + CONV_DIM, D_SSD + CONV_DIM + 2 * N_HEADS], axis=-1)
    xBC = jax.nn.silu(centred_depthwise_conv(xBC, conv_w, conv_b))
    xs, Bm, Cm = jnp.split(xBC, [D_SSD, D_SSD + N_BC_GROUPS * D_STATE], axis=-1)
    xs = xs.reshape(b, L, N_HEADS, HEADDIM)
    rep = N_HEADS // N_BC_GROUPS
    Bh = jnp.repeat(Bm.reshape(b, L, N_BC_GROUPS, D_STATE), rep, axis=2)
    Ch = jnp.repeat(Cm.reshape(b, L, N_BC_GROUPS, D_STATE), rep, axis=2)
    y_f, s_f = ssd_direction(xs, dt_raw[..., :N_HEADS], Bh, Ch, dt_bias[0], a_log[0], d_skip[0], init_f, False)
    y_b, s_b = ssd_direction(xs, dt_raw[..., N_HEADS:], Bh, Ch, dt_bias[1], a_log[1], d_skip[1], init_b, True)
    y = (y_f + y_b).reshape(b, L, D_SSD)
    y = rmsnorm(y * jax.nn.silu(z.astype(F32)), ssd_norm_w)
    u = pool_mixer(v, pool_w, pool_scale, grid)
    mix = jnp.concatenate([y, u], axis=-1) @ w_out
    x = x + gate1 * mix
    h2 = rmsnorm(x, norm2_w) * (1.0 + scale2) + shift2
    x = x + gate2 * routed_moe(h2, w_router, router_bias, w_gate, w_up, w_down)
    return x, s_f, s_b


def setup_inputs(seed: int = 0) -> dict:
    key = jax.random.key(seed)
    ks = jax.random.split(key, 26)

    def nrm(k, shape, scale):
        return jax.random.normal(k, shape, jnp.float32) * scale

    dt0 = jnp.exp(jax.random.uniform(ks[10], (DEPTH, 2, N_HEADS), jnp.float32)
                  * (math.log(0.1) - math.log(0.001)) + math.log(0.001))
    return {
        "x_prompt": nrm(ks[0], (BATCH, SEQ, D_MODEL), 1.0),
        "x_sample": nrm(ks[1], (DEC_BATCH, DEC_SEQ, D_MODEL), 1.0),
        "state_ssm": nrm(ks[2], (DEC_BATCH, DEPTH, 2, N_HEADS, HEADDIM, D_STATE), 0.1),
        "c": nrm(ks[3], (DEC_BATCH, D_MODEL), 1.0),
        "c_ctx": nrm(ks[4], (D_MODEL,), 1.0),
        "w_ada": nrm(ks[5], (DEPTH, D_MODEL, 6 * D_MODEL), 0.5 * D_MODEL ** -0.5),
        "b_ada": nrm(ks[6], (DEPTH, 6 * D_MODEL), 0.02),
        "norm1_w": 1.0 + nrm(ks[7], (DEPTH, D_MODEL), 0.02),
        "w_in": nrm(ks[8], (DEPTH, D_MODEL, D_IN_PROJ), D_MODEL ** -0.5),
        "conv_w": nrm(ks[9], (DEPTH, D_CONV, CONV_DIM), D_CONV ** -0.5),
        "conv_b": nrm(ks[11], (DEPTH, CONV_DIM), 0.02),
        "dt_bias": dt0 + jnp.log(-jnp.expm1(-dt0)),
        "a_log": jnp.log(jax.random.uniform(ks[12], (DEPTH, 2, N_HEADS), jnp.float32, 1.0, 16.0)),
        "d_skip": 1.0 + nrm(ks[13], (DEPTH, 2, N_HEADS), 0.1),
        "ssd_norm_w": 1.0 + nrm(ks[14], (DEPTH, D_SSD), 0.02),
        "pool_w": nrm(ks[15], (DEPTH, N_POOL_GROUPS, POOL_GROUP_W, POOL_GROUP_W), POOL_GROUP_W ** -0.5),
        "pool_scale": 1.0 + nrm(ks[16], (DEPTH, D_POOL), 0.1),
        "w_out": nrm(ks[17], (DEPTH, D_MIX, D_MODEL), D_MIX ** -0.5),
        "norm2_w": 1.0 + nrm(ks[18], (DEPTH, D_MODEL), 0.02),
        "w_router": nrm(ks[19], (D_MODEL, N_EXPERTS), D_MODEL ** -0.5),
        "router_bias": nrm(ks[20], (N_EXPERTS,), 0.01),
        "w_gate": nrm(ks[21], (DEPTH, N_EXPERTS, D_MODEL, D_FF_EXPERT), D_MODEL ** -0.5),
        "w_up": nrm(ks[22], (DEPTH, N_EXPERTS, D_MODEL, D_FF_EXPERT), D_MODEL ** -0.5),
        "w_down": nrm(ks[23], (DEPTH, N_EXPERTS, D_FF_EXPERT, D_MODEL), D_FF_EXPERT ** -0.5),
        "final_norm_w": 1.0 + nrm(ks[24], (D_MODEL,), 0.02),
    }


def reference(x_prompt, x_sample, state_ssm, c, c_ctx, w_ada, b_ada, norm1_w, w_in, conv_w, conv_b,
              dt_bias, a_log, d_skip, ssd_norm_w, pool_w, pool_scale, w_out, norm2_w,
              w_router, router_bias, w_gate, w_up, w_down, final_norm_w):
    def layer_weights(l):
        return (w_ada[l], b_ada[l], norm1_w[l], w_in[l], conv_w[l], conv_b[l], dt_bias[l], a_log[l],
                d_skip[l], ssd_norm_w[l], pool_w[l], pool_scale[l], w_out[l], norm2_w[l],
                w_router, router_bias, w_gate[l], w_up[l], w_down[l])

    zero_state = jnp.zeros((x_prompt.shape[0], N_HEADS, HEADDIM, D_STATE), F32)
    hp = x_prompt
    ctx_states = []
    for l in range(DEPTH):
        hp, s_f, s_b = trunk_layer(hp, c_ctx[None, :], zero_state, zero_state, False, *layer_weights(l))
        ctx_states.append(jnp.stack([s_f, s_b], axis=1))
    new_state_ssm = jnp.stack(ctx_states, axis=1)
    y_prompt = rmsnorm(hp, final_norm_w)

    hs = x_sample
    for l in range(DEPTH):
        hs, _, _ = trunk_layer(hs, c, state_ssm[:, l, 0], state_ssm[:, l, 1], True, *layer_weights(l))
    y_sample = rmsnorm(hs, final_norm_w)
    return (y_prompt, y_sample, new_state_ssm)
```

```python
import functools

import numpy as np
import jax
import jax.numpy as jnp
from jax import lax
from jax.experimental import pallas as pl
from jax.experimental.pallas import tpu as pltpu

F32 = jnp.float32
BF16 = jnp.bfloat16
I32 = jnp.int32

D_MODEL = 1024
HEADDIM = 64
N_HEADS = 16
N_BC_GROUPS = 2
HEADS_PER_BC = N_HEADS // N_BC_GROUPS
D_STATE = 128
CHUNK = 128
D_SSD = N_HEADS * HEADDIM
D_BC = N_BC_GROUPS * D_STATE
CONV_DIM = D_SSD + 2 * D_BC
POOL_WINDOWS = (2, 4, 8, 16)
POOL_GROUP_W = 256
GRID_W = 64
N_EXPERTS = 16
EXPERTS_PER_GROUP = 4
N_EXPERT_GROUPS = N_EXPERTS // EXPERTS_PER_GROUP
D_FF = 512
EPS = 1e-6

LANES = 128
HEAD_QUAD = 4
QUAD_W = HEAD_QUAD * HEADDIM
EXPERT_PAIRS = ((0, 1), (0, 2), (0, 3), (1, 2), (1, 3), (2, 3))
N_CLASSES = N_EXPERT_GROUPS * len(EXPERT_PAIRS)

TM_PROJ = 512
TM_TOK = 256
TM_EXP = 256
GATE_LANES = 128
ROW_W = D_MODEL + GATE_LANES
DMA_UNROLL = 8
NEG_INF = float("-inf")


def _dot(a, b):
    return jnp.dot(a, b, preferred_element_type=F32)


def _split_bf16(x):
    hi = x.astype(BF16)
    lo = (x - hi.astype(F32)).astype(BF16)
    return hi, lo


def _dot_split(a, w):
    ah, al = _split_bf16(a)
    wh, wl = _split_bf16(w)
    return _dot(ah, wh) + (_dot(al, wh) + _dot(ah, wl))


def _silu(x):
    return x / (1.0 + jnp.exp(-x))


def _rms(x):
    return x * lax.rsqrt(jnp.mean(x * x, axis=-1, keepdims=True) + EPS)


def _ada_kernel(c_ref, w_ref, b_ref, o_ref):
    o_ref[0] = _dot_split(_silu(c_ref[...]), w_ref[0]) + b_ref[0]


def _ada_call(cvec, w_ada, b_ada):
    depth, d, n = w_ada.shape
    tn = 1024
    return pl.pallas_call(
        _ada_kernel,
        grid=(depth, n // tn),
        in_specs=[
            pl.BlockSpec((8, d), lambda l, j: (0, 0)),
            pl.BlockSpec((1, d, tn), lambda l, j: (l, 0, j)),
            pl.BlockSpec((1, 1, tn), lambda l, j: (l, 0, j)),
        ],
        out_specs=pl.BlockSpec((1, 8, tn), lambda l, j: (l, 0, j)),
        out_shape=jax.ShapeDtypeStruct((depth, 8, n), F32),
        name="ada_ln",
    )(cvec, w_ada, b_ada.reshape(depth, 1, n))


def _inproj_kernel(x_ref, mod_ref, nw_ref, wz_ref, wx_ref, wv_ref, wd_ref,
                   z_ref, xbc_ref, v_ref, dt_ref):
    h = _rms(x_ref[...]) * nw_ref[...]
    h = h * (1.0 + mod_ref[0, 1:2, :]) + mod_ref[0, 0:1, :]
    hb = h.astype(BF16)
    z_ref[...] = _dot(hb, wz_ref[...]).astype(BF16)
    xbc_ref[...] = _dot(hb, wx_ref[...])
    v_ref[...] = _dot(hb, wv_ref[...]).astype(BF16)
    dt_ref[...] = _dot(hb, wd_ref[...])


def _inproj_call(x, mod, nw, wz, wx, wv, wd, mod_row):
    t = x.shape[0]
    tm = TM_PROJ
    const = lambda i: (0, 0)
    row = lambda i: (i, 0)
    return pl.pallas_call(
        _inproj_kernel,
        grid=(t // tm,),
        in_specs=[
            pl.BlockSpec((tm, D_MODEL), row),
            pl.BlockSpec((1, 8, D_MODEL), lambda i: (mod_row(i * tm), 0, 0)),
            pl.BlockSpec((1, D_MODEL), const),
            pl.BlockSpec(wz.shape, const),
            pl.BlockSpec(wx.shape, const),
            pl.BlockSpec(wv.shape, const),
            pl.BlockSpec(wd.shape, const),
        ],
        out_specs=[
            pl.BlockSpec((tm, D_SSD), row),
            pl.BlockSpec((tm, CONV_DIM), row),
            pl.BlockSpec((tm, D_MODEL), row),
            pl.BlockSpec((tm, LANES), row),
        ],
        out_shape=[
            jax.ShapeDtypeStruct((t, D_SSD), BF16),
            jax.ShapeDtypeStruct((t, CONV_DIM), F32),
            jax.ShapeDtypeStruct((t, D_MODEL), BF16),
            jax.ShapeDtypeStruct((t, LANES), F32),
        ],
        name="in_proj",
    )(x, mod, nw, wz, wx, wv, wd)


def _expand_heads(val, e_ref):
    hi, lo = _split_bf16(val)
    return _dot(hi, e_ref[...]) + _dot(lo, e_ref[...])


def _ssd_kernel(*refs, nc, has_init, has_final):
    it = iter(refs)
    xm_ref, xp_ref, xn_ref, dt_ref, z_ref = (next(it) for _ in range(5))
    init_ref = next(it) if has_init else None
    cw_ref, cb_ref, dtb_ref, alog_ref, dsk_ref, nw_ref, e_ref = (next(it) for _ in range(7))
    y_ref = next(it)
    fin_ref = next(it) if has_final else None
    st_ref, yb_ref = next(it), next(it)

    ph = pl.program_id(1)
    c = pl.program_id(2)
    fwd = ph == 1
    cc = jnp.where(fwd, c, nc - 1 - c)
    q = CHUNK

    @pl.when(c == 0)
    def _():
        if has_init:
            st_ref[...] = init_ref[0, 0]
        else:
            st_ref[...] = jnp.zeros_like(st_ref)

    xm = xm_ref[...]
    prev_row = jnp.where(cc > 0, xp_ref[7:8, :], 0.0)
    next0 = jnp.where(cc < nc - 1, xn_ref[0:1, :], 0.0)
    next1 = jnp.where(cc < nc - 1, xn_ref[1:2, :], 0.0)
    rows = lax.broadcasted_iota(I32, (q, 1), 0)
    x_m1 = jnp.where(rows == 0, prev_row, pltpu.roll(xm, 1, 0))
    x_p1 = jnp.where(rows == q - 1, next0, pltpu.roll(xm, q - 1, 0))
    x_p2 = jnp.where(rows == q - 2, next0, jnp.where(rows == q - 1, next1, pltpu.roll(xm, q - 2, 0)))
    conv = (cw_ref[0:1, :] * x_m1 + cw_ref[1:2, :] * xm + cw_ref[2:3, :] * x_p1
            + cw_ref[3:4, :] * x_p2 + cb_ref[...])
    act = _silu(conv)
    xs = act[:, :D_SSD]
    xs_b = xs.astype(BF16)
    b_parts = [act[:, D_SSD + g * D_STATE:D_SSD + (g + 1) * D_STATE] for g in range(N_BC_GROUPS)]
    c_parts = [act[:, D_SSD + D_BC + g * D_STATE:D_SSD + D_BC + (g + 1) * D_STATE].astype(BF16)
               for g in range(N_BC_GROUPS)]
    bt_parts = [b.T for b in b_parts]
    g_parts = [lax.dot_general(c_parts[g], b_parts[g].astype(BF16), (((1,), (1,)), ((), ())),
                               preferred_element_type=F32) for g in range(N_BC_GROUPS)]

    dt_t = dt_ref[...].T
    dsel = jnp.where(fwd, dt_t[0:N_HEADS, :], dt_t[N_HEADS:2 * N_HEADS, :])
    dtb = jnp.where(fwd, dtb_ref[0:N_HEADS, :], dtb_ref[N_HEADS:2 * N_HEADS, :])
    alog = jnp.where(fwd, alog_ref[0:N_HEADS, :], alog_ref[N_HEADS:2 * N_HEADS, :])
    xdt = dsel + dtb
    dtv = jnp.maximum(xdt, 0.0) + jnp.log(1.0 + jnp.exp(-jnp.abs(xdt)))
    a_step = dtv * (-jnp.exp(alog))
    lane = lax.broadcasted_iota(I32, (N_HEADS, q), 1)
    pref = a_step
    k = 1
    while k < q:
        pref = pref + jnp.where(lane >= k, pltpu.roll(pref, k, 1), 0.0)
        k *= 2
    total = pref[:, q - 1:q]
    acum = jnp.where(fwd, pref, total - pref + a_step)
    w_t = dtv * jnp.exp(total - acum)
    e_acum = jnp.exp(acum)
    cols = jnp.concatenate([acum, e_acum, jnp.zeros((q - 2 * N_HEADS, q), F32)], axis=0).T
    acum_c = cols[:, 0:N_HEADS]
    e_acum_c = cols[:, N_HEADS:2 * N_HEADS]

    li = lax.broadcasted_iota(I32, (q, q), 0)
    si = lax.broadcasted_iota(I32, (q, q), 1)
    dls = li - si
    causal = jnp.where(fwd, dls, -dls) >= 0

    lane_q = lax.broadcasted_iota(I32, (q, QUAD_W), 1) // HEADDIM
    y_parts, s_parts = [], []
    for qd in range(N_HEADS // HEAD_QUAD):
        top, bot = [], []
        for j in range(HEAD_QUAD):
            h = qd * HEAD_QUAD + j
            g = h // HEADS_PER_BC
            seg = acum_c[:, h:h + 1] - acum[h:h + 1, :]
            dec = jnp.where(causal, jnp.exp(seg), 0.0)
            top.append((g_parts[g] * dec * dtv[h:h + 1, :]).astype(BF16))
            bot.append((bt_parts[g] * w_t[h:h + 1, :]).astype(BF16))
        lhs = jnp.concatenate([jnp.concatenate(top, axis=1), jnp.concatenate(bot, axis=1)], axis=0)
        xq = xs_b[:, qd * QUAD_W:(qd + 1) * QUAD_W]
        rhs = jnp.concatenate([jnp.where(lane_q == j, xq, jnp.zeros_like(xq)) for j in range(HEAD_QUAD)], axis=0)
        res = _dot(lhs, rhs)
        y_parts.append(res[:q])
        s_parts.append(res[q:])
    y_diag = jnp.concatenate(y_parts, axis=1)
    st_new = jnp.concatenate(s_parts, axis=1)

    st = st_ref[...]
    st_b = st.astype(BF16)
    half = D_SSD // N_BC_GROUPS
    y_off = jnp.concatenate([_dot(c_parts[g], st_b[:, g * half:(g + 1) * half]) for g in range(N_BC_GROUPS)],
                            axis=1)
    y_dir = y_diag + y_off * _expand_heads(e_acum_c, e_ref)
    e_tot = jnp.sum(jnp.exp(total) * e_ref[...].astype(F32), axis=0, keepdims=True)
    st_next = st * e_tot + st_new
    st_ref[...] = st_next

    if has_final:
        @pl.when(c == nc - 1)
        def _():
            fin_ref[0, 0] = st_next

    r0 = pl.multiple_of(cc * q, q)

    @pl.when(ph == 0)
    def _():
        yb_ref[pl.ds(r0, q), :] = y_dir

    @pl.when(ph == 1)
    def _():
        y = y_dir + yb_ref[pl.ds(r0, q), :] + xs * (dsk_ref[0:1, :] + dsk_ref[1:2, :])
        y = y * _silu(z_ref[...].astype(F32))
        y_ref[...] = (_rms(y) * nw_ref[...]).astype(BF16)


def _ssd_call(xbc, dt, z, init, cw, cb, dtb, alog, dsk, nw, e_mat, *, n_seq, nc, base, has_final):
    t = xbc.shape[0]
    q = CHUNK
    has_init = init is not None
    last8 = t // 8 - 1

    def chunk_of(s, ph, c):
        return base + s * nc + jnp.where(ph == 1, c, nc - 1 - c)

    main = lambda s, ph, c: (chunk_of(s, ph, c), 0)
    prev = lambda s, ph, c: (jnp.maximum(chunk_of(s, ph, c) * (q // 8) - 1, 0), 0)
    nxt = lambda s, ph, c: (jnp.minimum((chunk_of(s, ph, c) + 1) * (q // 8), last8), 0)
    const = lambda s, ph, c: (0, 0)

    in_specs = [
        pl.BlockSpec((q, CONV_DIM), main),
        pl.BlockSpec((8, CONV_DIM), prev),
        pl.BlockSpec((8, CONV_DIM), nxt),
        pl.BlockSpec((q, LANES), main),
        pl.BlockSpec((q, D_SSD), lambda s, ph, c: (base + s * nc + ph * c, 0)),
    ]
    args = [xbc, xbc, xbc, dt, z]
    if has_init:
        in_specs.append(pl.BlockSpec((1, 1, D_STATE, D_SSD), lambda s, ph, c: (s, 1 - ph, 0, 0)))
        args.append(init)
    for a in (cw, cb, dtb, alog, dsk, nw, e_mat):
        in_specs.append(pl.BlockSpec(a.shape, const))
        args.append(a)

    out_specs = [pl.BlockSpec((q, D_SSD), lambda s, ph, c: (s * nc + ph * c, 0))]
    out_shape = [jax.ShapeDtypeStruct((n_seq * nc * q, D_SSD), BF16)]
    if has_final:
        out_specs.append(pl.BlockSpec((1, 1, D_STATE, D_SSD), lambda s, ph, c: (s, 1 - ph, 0, 0)))
        out_shape.append(jax.ShapeDtypeStruct((n_seq, 2, D_STATE, D_SSD), F32))

    out = pl.pallas_call(
        functools.partial(_ssd_kernel, nc=nc, has_init=has_init, has_final=has_final),
        grid=(n_seq, 2, nc),
        in_specs=in_specs,
        out_specs=out_specs,
        out_shape=out_shape,
        scratch_shapes=[pltpu.VMEM((D_STATE, D_SSD), F32), pltpu.VMEM((nc * q, D_SSD), F32)],
        compiler_params=pltpu.CompilerParams(dimension_semantics=("arbitrary", "arbitrary", "arbitrary")),
        name="ssd_mixer_nc%d" % nc,
    )(*args)
    return out if has_final else (out[0], None)


def _window_count(idx, w, n):
    hi = jnp.minimum(idx + (w // 2 - 1), n - 1)
    lo = jnp.maximum(idx - w // 2, 0)
    return hi - lo + 1


def _pool_tail(s, cnt, vt, g, pw_ref, ps_ref):
    cnt2 = jnp.concatenate([cnt, cnt], axis=1).astype(F32)
    upre = s / cnt2 - vt.astype(F32)
    gs = slice(g * POOL_GROUP_W, (g + 1) * POOL_GROUP_W)
    return (_dot(upre.astype(BF16), pw_ref[g]) * ps_ref[:, gs]).astype(BF16)


def _pool_seq_kernel(v_ref, band_ref, pw_ref, ps_ref, u_ref):
    n = v_ref.shape[0]
    pos = lax.broadcasted_iota(I32, (n, LANES), 0)
    for g, w in enumerate(POOL_WINDOWS):
        gs = slice(g * POOL_GROUP_W, (g + 1) * POOL_GROUP_W)
        vt = v_ref[:, gs]
        s = _dot(band_ref[g], vt)
        u_ref[:, gs] = _pool_tail(s, _window_count(pos, w, n), vt, g, pw_ref, ps_ref)


def _pool_grid_kernel(v_ref, band_ref, pw_ref, ps_ref, u_ref, p_ref, *, rows):
    tile = 256
    n = rows * GRID_W
    pad = (max(POOL_WINDOWS) // 2) * GRID_W
    p_ref[0:pad, :] = jnp.zeros((pad, POOL_GROUP_W), F32)
    p_ref[pad + n:pad + n + pad, :] = jnp.zeros((pad, POOL_GROUP_W), F32)
    for g, w in enumerate(POOL_WINDOWS):
        gs = slice(g * POOL_GROUP_W, (g + 1) * POOL_GROUP_W)

        def colsum(t, carry):
            r0 = pl.multiple_of(t * tile, tile)
            p_ref[pl.ds(pad + r0, tile), :] = _dot(band_ref[g], v_ref[pl.ds(r0, tile), gs])
            return carry

        lax.fori_loop(0, n // tile, colsum, 0)

        def rowsum(t, carry):
            r0 = pl.multiple_of(t * tile, tile)
            s = p_ref[pl.ds(pad + r0 - (w // 2) * GRID_W, tile), :]
            for d in range(-(w // 2) + 1, w // 2):
                s = s + p_ref[pl.ds(pad + r0 + d * GRID_W, tile), :]
            tok = r0 + lax.broadcasted_iota(I32, (tile, LANES), 0)
            grid_row = jnp.right_shift(tok, GRID_W.bit_length() - 1)
            grid_col = jnp.bitwise_and(tok, GRID_W - 1)
            cnt = _window_count(grid_row, w, rows) * _window_count(grid_col, w, GRID_W)
            vt = v_ref[pl.ds(r0, tile), gs]
            u_ref[pl.ds(r0, tile), gs] = _pool_tail(s, cnt, vt, g, pw_ref, ps_ref)
            return carry

        lax.fori_loop(0, n // tile, rowsum, 0)


def _band_matrix(n, w):
    idx = np.arange(n)
    lo = np.clip(idx - w // 2, 0, n - 1)
    hi = np.clip(idx + w // 2 - 1, 0, n - 1)
    j = idx[None, :]
    return ((j >= lo[:, None]) & (j <= hi[:, None])).astype(np.float32)


def _pool_call(v, pw, ps, *, n_seq, seq_len, base_block, grid_rows):
    if grid_rows is None:
        bands = np.stack([_band_matrix(seq_len, w) for w in POOL_WINDOWS])
        kern = _pool_seq_kernel
        scratch = []
    else:
        per = 256 // GRID_W
        bands = np.stack([np.kron(np.eye(per, dtype=np.float32), _band_matrix(GRID_W, w)) for w in POOL_WINDOWS])
        kern = functools.partial(_pool_grid_kernel, rows=grid_rows)
        pad = (max(POOL_WINDOWS) // 2) * GRID_W
        scratch = [pltpu.VMEM((seq_len + 2 * pad, POOL_GROUP_W), F32)]
    bands = jnp.asarray(bands, BF16)
    const3 = lambda s: (0, 0, 0)
    return pl.pallas_call(
        kern,
        grid=(n_seq,),
        in_specs=[
            pl.BlockSpec((seq_len, D_MODEL), lambda s: (base_block + s, 0)),
            pl.BlockSpec(bands.shape, const3),
            pl.BlockSpec(pw.shape, const3),
            pl.BlockSpec(ps.shape, lambda s: (0, 0)),
        ],
        out_specs=pl.BlockSpec((seq_len, D_MODEL), lambda s: (s, 0)),
        out_shape=jax.ShapeDtypeStruct((n_seq * seq_len, D_MODEL), BF16),
        scratch_shapes=scratch,
        name="pool_mixer_len%d" % seq_len,
    )(v, bands, pw, ps)


def _route_kernel(yp_ref, ys_ref, up_ref, us_ref, x_ref, mod_ref, wy_ref, wu_ref, nw_ref, wr_ref, rb_ref,
                  x1_ref, h2_ref, cls_ref, *, prompt_tiles):
    tm = x_ref.shape[0]
    is_prompt = pl.program_id(0) < prompt_tiles
    y = jnp.where(is_prompt, yp_ref[...], ys_ref[...])
    u = jnp.where(is_prompt, up_ref[...], us_ref[...])
    mix = _dot(y, wy_ref[...]) + _dot(u, wu_ref[...])
    x1 = x_ref[...] + mod_ref[0, 2:3, :] * mix
    x1_ref[...] = x1
    h2 = (_rms(x1) * nw_ref[...]) * (1.0 + mod_ref[0, 4:5, :]) + mod_ref[0, 3:4, :]
    h2_ref[:, 0:D_MODEL] = h2

    logits = _dot_split(h2, wr_ref[...]).T[0:N_EXPERTS, :]
    e = jnp.exp(logits - jnp.max(logits, axis=0, keepdims=True))
    probs = e / jnp.sum(e, axis=0, keepdims=True)
    sel = probs + rb_ref[...]
    s_rows = [sel[i:i + 1, :] for i in range(N_EXPERTS)]
    p_rows = [probs[i:i + 1, :] for i in range(N_EXPERTS)]

    scores = []
    for g in range(N_EXPERT_GROUPS):
        a, b, c, d = s_rows[4 * g:4 * g + 4]
        hi1, lo1 = jnp.maximum(a, b), jnp.minimum(a, b)
        hi2, lo2 = jnp.maximum(c, d), jnp.minimum(c, d)
        scores.append(jnp.maximum(hi1, hi2) + jnp.maximum(jnp.minimum(hi1, hi2), jnp.maximum(lo1, lo2)))
    best_g = jnp.zeros((1, tm), I32)
    best_s = scores[0]
    for g in range(1, N_EXPERT_GROUPS):
        better = scores[g] > best_s
        best_g = jnp.where(better, g, best_g)
        best_s = jnp.where(better, scores[g], best_s)

    def in_group(rows, j):
        out = rows[j]
        for g in range(1, N_EXPERT_GROUPS):
            out = jnp.where(best_g == g, rows[4 * g + j], out)
        return out

    sg = [in_group(s_rows, j) for j in range(EXPERTS_PER_GROUP)]
    pg = [in_group(p_rows, j) for j in range(EXPERTS_PER_GROUP)]
    i1 = jnp.zeros((1, tm), I32)
    v1 = sg[0]
    for j in range(1, EXPERTS_PER_GROUP):
        better = sg[j] > v1
        i1 = jnp.where(better, j, i1)
        v1 = jnp.where(better, sg[j], v1)
    i2 = jnp.zeros((1, tm), I32)
    v2 = jnp.where(i1 == 0, NEG_INF, sg[0])
    for j in range(1, EXPERTS_PER_GROUP):
        cand = jnp.where(i1 == j, NEG_INF, sg[j])
        better = cand > v2
        i2 = jnp.where(better, j, i2)
        v2 = jnp.where(better, cand, v2)

    def pick(idx):
        out = pg[0]
        for j in range(1, EXPERTS_PER_GROUP):
            out = jnp.where(idx == j, pg[j], out)
        return out

    p1, p2 = pick(i1), pick(i2)
    psum = p1 + p2
    g1, g2 = p1 / psum, p2 / psum
    lo = jnp.minimum(i1, i2)
    hi = jnp.maximum(i1, i2)
    g_lo = jnp.where(i1 < i2, g1, g2)
    g_hi = jnp.where(i1 < i2, g2, g1)
    pair = jnp.where(lo == 0, hi - 1, jnp.where(lo == 1, hi + 1, 5))
    cls_ref[0] = best_g * len(EXPERT_PAIRS) + pair

    r = lax.broadcasted_iota(I32, (GATE_LANES, tm), 0)
    gates_t = jnp.where(r == 0, g_lo, jnp.where(r == 1, g_hi, 0.0))
    h2_ref[:, D_MODEL:ROW_W] = gates_t.T


def _route_call(yp, ys, up, us, x, mod, wy, wu, nw, wr, rb, mod_row):
    t = x.shape[0]
    tm = TM_TOK
    pt = yp.shape[0] // tm
    st = ys.shape[0] // tm
    row = lambda i: (i, 0)
    const = lambda i: (0, 0)
    prow = lambda i: (jnp.minimum(i, pt - 1), 0)
    srow = lambda i: (jnp.clip(i - pt, 0, st - 1), 0)
    return pl.pallas_call(
        functools.partial(_route_kernel, prompt_tiles=pt),
        grid=(t // tm,),
        in_specs=[
            pl.BlockSpec((tm, D_SSD), prow),
            pl.BlockSpec((tm, D_SSD), srow),
            pl.BlockSpec((tm, D_MODEL), prow),
            pl.BlockSpec((tm, D_MODEL), srow),
            pl.BlockSpec((tm, D_MODEL), row),
            pl.BlockSpec((1, 8, D_MODEL), lambda i: (mod_row(i * tm), 0, 0)),
            pl.BlockSpec(wy.shape, const),
            pl.BlockSpec(wu.shape, const),
            pl.BlockSpec((1, D_MODEL), const),
            pl.BlockSpec(wr.shape, const),
            pl.BlockSpec(rb.shape, const),
        ],
        out_specs=[
            pl.BlockSpec((tm, D_MODEL), row),
            pl.BlockSpec((tm, ROW_W), row),
            pl.BlockSpec((1, 1, tm), lambda i: (i, 0, 0)),
        ],
        out_shape=[
            jax.ShapeDtypeStruct((t, D_MODEL), F32),
            jax.ShapeDtypeStruct((t, ROW_W), F32),
            jax.ShapeDtypeStruct((t // tm, 1, tm), I32),
        ],
        name="out_proj_router",
    )(yp, ys, up, us, x, mod, wy, wu, nw, wr, rb)


def _sort_kernel(cls_ref, pos_ref, tile_ref):
    cls = cls_ref[...]
    nr = cls.shape[0]
    ci = lax.broadcasted_iota(I32, (LANES, LANES), 0)
    cj = lax.broadcasted_iota(I32, (LANES, LANES), 1)
    upper = jnp.where(ci <= cj, 1.0, 0.0).astype(BF16)
    ri = lax.broadcasted_iota(I32, (nr, nr), 0)
    rj = lax.broadcasted_iota(I32, (nr, nr), 1)
    lower = jnp.where(rj < ri, 1.0, 0.0).astype(BF16)
    tile_idx = lax.broadcasted_iota(I32, (8, LANES), 1).astype(F32)
    pos = jnp.zeros(cls.shape, F32)
    tile_cls = jnp.zeros((8, LANES), F32)
    tiles_done = jnp.zeros((1, 1), F32)
    for k in range(N_CLASSES):
        member = cls == k
        mf = jnp.where(member, 1.0, 0.0)
        within = _dot(mf.astype(BF16), upper)
        row_tot = jnp.broadcast_to(within[:, LANES - 1:LANES], cls.shape)
        before = _dot(lower, row_tot.astype(BF16))
        rank = within - mf + before
        count = jnp.sum(jnp.sum(mf, axis=1, keepdims=True), axis=0, keepdims=True)
        n_tiles = jnp.floor((count + (TM_EXP - 1)) * (1.0 / TM_EXP))
        pos = pos + jnp.where(member, tiles_done * TM_EXP + rank, 0.0)
        tiles_done = tiles_done + n_tiles
        tile_cls = tile_cls + jnp.where(tile_idx >= tiles_done, 1.0, 0.0)
    pos_ref[...] = pos.astype(I32)
    r = lax.broadcasted_iota(I32, (8, LANES), 0)
    tile_ref[...] = jnp.where(r == 1, tiles_done, tile_cls).astype(I32)


def _sort_call(cls):
    t = cls.size
    rows = -(-t // (LANES * LANES)) * LANES
    cls2 = jnp.pad(cls.reshape(t // LANES, LANES), ((0, rows - t // LANES), (0, 0)), constant_values=N_CLASSES)
    pos, tiles = pl.pallas_call(
        _sort_kernel,
        out_shape=[jax.ShapeDtypeStruct(cls2.shape, I32), jax.ShapeDtypeStruct((8, LANES), I32)],
        name="route_sort",
    )(cls2)
    return pos.reshape(-1)[:t], tiles


def _scatter_kernel(pos_ref, src_ref, dst_in, dst_ref, sem):
    del dst_in
    tm = src_ref.shape[0]
    base = pl.program_id(0) * tm

    def row_copy(r):
        return pltpu.make_async_copy(src_ref.at[pl.ds(r, 1), :], dst_ref.at[pl.ds(pos_ref[base + r], 1), :], sem)

    def body(r, carry):
        row_copy(r).start()
        return carry

    lax.fori_loop(0, tm, body, 0, unroll=DMA_UNROLL)

    def drain(r, carry):
        row_copy(r).wait()
        return carry

    lax.fori_loop(0, tm, drain, 0, unroll=DMA_UNROLL)


def _scatter_call(pos, src, n_rows):
    t, w = src.shape
    tm = TM_TOK
    dst = jnp.zeros((n_rows, w), src.dtype)
    return pl.pallas_call(
        _scatter_kernel,
        grid_spec=pltpu.PrefetchScalarGridSpec(
            num_scalar_prefetch=1,
            grid=(t // tm,),
            in_specs=[pl.BlockSpec((tm, w), lambda i, pos: (i, 0)),
                      pl.BlockSpec(memory_space=pl.ANY)],
            out_specs=pl.BlockSpec(memory_space=pl.ANY),
            scratch_shapes=[pltpu.SemaphoreType.DMA(())],
        ),
        out_shape=jax.ShapeDtypeStruct(dst.shape, dst.dtype),
        input_output_aliases={2: 0},
        name="route_scatter",
    )(pos, src, dst)


def _unpermute_kernel(pos_ref, x1_ref, mod_ref, nw_ref, y_hbm, o_ref, buf, sem, *, final_norm):
    tm = x1_ref.shape[0]
    base = pl.program_id(0) * tm

    def row_copy(r):
        return pltpu.make_async_copy(y_hbm.at[pl.ds(pos_ref[base + r], 1), :], buf.at[pl.ds(r, 1), :], sem)

    def body(r, carry):
        row_copy(r).start()
        return carry

    lax.fori_loop(0, tm, body, 0, unroll=DMA_UNROLL)

    def drain(r, carry):
        row_copy(r).wait()
        return carry

    lax.fori_loop(0, tm, drain, 0, unroll=DMA_UNROLL)
    x2 = x1_ref[...] + mod_ref[0, 5:6, :] * buf[...]
    if final_norm:
        x2 = _rms(x2) * nw_ref[...]
    o_ref[...] = x2


def _unpermute_call(pos, x1, mod, nw, y_sorted, mod_row, final_norm):
    t = x1.shape[0]
    tm = TM_TOK
    return pl.pallas_call(
        functools.partial(_unpermute_kernel, final_norm=final_norm),
        grid_spec=pltpu.PrefetchScalarGridSpec(
            num_scalar_prefetch=1,
            grid=(t // tm,),
            in_specs=[pl.BlockSpec((tm, D_MODEL), lambda i, pos: (i, 0)),
                      pl.BlockSpec((1, 8, D_MODEL), lambda i, pos: (mod_row(i * tm), 0, 0)),
                      pl.BlockSpec((1, D_MODEL), lambda i, pos: (0, 0)),
                      pl.BlockSpec(memory_space=pl.ANY)],
            out_specs=pl.BlockSpec((tm, D_MODEL), lambda i, pos: (i, 0)),
            scratch_shapes=[pltpu.VMEM((tm, D_MODEL), F32), pltpu.SemaphoreType.DMA(())],
        ),
        out_shape=jax.ShapeDtypeStruct((t, D_MODEL), F32),
        name="moe_unpermute",
    )(pos, x1, mod, nw, y_sorted)


def _moe_kernel(ea_ref, eb_ref, nact_ref, x_ref, wga, wua, wda, wgb, wub, wdb, y_ref):
    del ea_ref, eb_ref
    j = pl.program_id(0)

    @pl.when(j < nact_ref[0])
    def _():
        x = x_ref[:, 0:D_MODEL].astype(BF16)
        gate_a = x_ref[:, D_MODEL:D_MODEL + 1]
        gate_b = x_ref[:, D_MODEL + 1:D_MODEL + 2]

        def ffn(wg, wu, wd, gate):
            g = _dot(x, wg[0].astype(BF16))
            u = _dot(x, wu[0].astype(BF16))
            a = _silu(g) * u * gate
            return _dot(a.astype(BF16), wd[0].astype(BF16))

        y_ref[...] = ffn(wga, wua, wda, gate_a) + ffn(wgb, wub, wdb, gate_b)

    @pl.when(j >= nact_ref[0])
    def _():
        y_ref[...] = jnp.zeros_like(y_ref)


def _moe_call(ea, eb, nact, xs, w_gate, w_up, w_down):
    n_rows = xs.shape[0]
    tm = TM_EXP
    wa = lambda j, ea, eb, na: (ea[j], 0, 0)
    wb = lambda j, ea, eb, na: (eb[j], 0, 0)
    gu = (1, D_MODEL, D_FF)
    dn = (1, D_FF, D_MODEL)
    return pl.pallas_call(
        _moe_kernel,
        grid_spec=pltpu.PrefetchScalarGridSpec(
            num_scalar_prefetch=3,
            grid=(n_rows // tm,),
            in_specs=[pl.BlockSpec((tm, ROW_W), lambda j, ea, eb, na: (j, 0)),
                      pl.BlockSpec(gu, wa), pl.BlockSpec(gu, wa), pl.BlockSpec(dn, wa),
                      pl.BlockSpec(gu, wb), pl.BlockSpec(gu, wb), pl.BlockSpec(dn, wb)],
            out_specs=pl.BlockSpec((tm, D_MODEL), lambda j, ea, eb, na: (j, 0)),
        ),
        out_shape=jax.ShapeDtypeStruct((n_rows, D_MODEL), F32),
        name="moe_ffn",
    )(ea, eb, nact, xs, w_gate, w_up, w_down, w_gate, w_up, w_down)


def kernel(x_prompt, x_sample, state_ssm, c, c_ctx, w_ada, b_ada, norm1_w, w_in, conv_w, conv_b, dt_bias, a_log, d_skip, ssd_norm_w, pool_w, pool_scale, w_out, norm2_w, w_router, router_bias, w_gate, w_up, w_down, final_norm_w):
    n_p, len_p, d = x_prompt.shape
    n_s, len_s, _ = x_sample.shape
    depth = w_ada.shape[0]
    t_p, t_s = n_p * len_p, n_s * len_s
    t = t_p + t_s
    assert d == D_MODEL and len_p % TM_TOK == 0 and len_s % TM_PROJ == 0 and t_p % TM_PROJ == 0
    assert len_s % GRID_W == 0 and t_p % len_s == 0

    def mod_row(tok):
        return jnp.where(tok < t_p, 0, 1 + (tok - t_p) // len_s)

    x = jnp.concatenate([x_prompt.reshape(t_p, d), x_sample.reshape(t_s, d)], axis=0)

    cvec = jnp.zeros((8, d), F32).at[0].set(c_ctx).at[1:1 + n_s].set(c)
    ada = _ada_call(cvec, w_ada, b_ada)
    mods = jnp.pad(ada[:, :1 + n_s].reshape(depth, 1 + n_s, 6, d), ((0, 0), (0, 0), (0, 2), (0, 0)))

    e_mat = jnp.asarray(np.kron(np.eye(N_HEADS, dtype=np.float32), np.ones((1, HEADDIM), np.float32)), BF16)
    wr = jnp.pad(w_router, ((0, 0), (0, LANES - N_EXPERTS)))
    rb = jnp.broadcast_to(router_bias[:, None], (N_EXPERTS, TM_TOK))
    pairs = np.asarray(EXPERT_PAIRS, np.int32)
    n_sorted = t + N_CLASSES * TM_EXP
    n_tiles = n_sorted // TM_EXP
    assert n_tiles <= LANES

    init_all = jnp.transpose(state_ssm, (1, 0, 2, 5, 3, 4)).reshape(depth, n_s, 2, D_STATE, D_SSD)

    new_states = []
    for l in range(depth):
        w_l = w_in[l]
        wz = w_l[:, :D_SSD].astype(BF16)
        wx = w_l[:, D_SSD:D_SSD + CONV_DIM].astype(BF16)
        o = D_SSD + CONV_DIM
        wd = jnp.pad(w_l[:, o:o + 2 * N_HEADS], ((0, 0), (0, LANES - 2 * N_HEADS))).astype(BF16)
        wv = w_l[:, o + 2 * N_HEADS:].astype(BF16)
        z, xbc, v, dt = _inproj_call(x, mods[l], norm1_w[l][None], wz, wx, wv, wd, mod_row)

        dtb = jnp.broadcast_to(dt_bias[l].reshape(2 * N_HEADS, 1), (2 * N_HEADS, CHUNK))
        alog = jnp.broadcast_to(a_log[l].reshape(2 * N_HEADS, 1), (2 * N_HEADS, CHUNK))
        dsk = jnp.repeat(d_skip[l], HEADDIM, axis=1)
        ssd_args = (conv_w[l], conv_b[l][None], dtb, alog, dsk, ssd_norm_w[l][None], e_mat)
        y_p, fin = _ssd_call(xbc, dt, z, None, *ssd_args, n_seq=n_p, nc=len_p // CHUNK, base=0, has_final=True)
        y_s, _ = _ssd_call(xbc, dt, z, init_all[l], *ssd_args, n_seq=n_s, nc=len_s // CHUNK,
                           base=t_p // CHUNK, has_final=False)
        new_states.append(fin)

        pw = pool_w[l].astype(BF16)
        ps = pool_scale[l][None]
        u_p = _pool_call(v, pw, ps, n_seq=n_p, seq_len=len_p, base_block=0, grid_rows=None)
        u_s = _pool_call(v, pw, ps, n_seq=n_s, seq_len=len_s, base_block=t_p // len_s, grid_rows=len_s // GRID_W)

        wy = w_out[l][:D_SSD].astype(BF16)
        wu = w_out[l][D_SSD:].astype(BF16)
        x1, h2, cls = _route_call(y_p, y_s, u_p, u_s, x, mods[l], wy, wu, norm2_w[l][None], wr, rb, mod_row)

        pos, tiles = _sort_call(cls)
        nact = tiles[1, 0:1]
        tcls = tiles[0, :n_tiles]
        tcls = jnp.minimum(tcls, tcls[jnp.maximum(nact[0] - 1, 0)])
        tcls = jnp.minimum(tcls, N_CLASSES - 1)
        grp = tcls // len(EXPERT_PAIRS)
        pr = tcls % len(EXPERT_PAIRS)
        ea = grp * EXPERTS_PER_GROUP + jnp.asarray(pairs[:, 0])[pr]
        eb = grp * EXPERTS_PER_GROUP + jnp.asarray(pairs[:, 1])[pr]

        h2s = _scatter_call(pos, h2, n_sorted)
        ys = _moe_call(ea, eb, nact, h2s, w_gate[l], w_up[l], w_down[l])
        last = l == depth - 1
        x = _unpermute_call(pos, x1, mods[l], final_norm_w[None], ys, mod_row, last)

    y_prompt = x[:t_p].reshape(n_p, len_p, d)
    y_sample = x[t_p:].reshape(n_s, len_s, d)
    ns = jnp.stack(new_states, axis=1)
    ns = ns.reshape(n_p, depth, 2, D_STATE, N_HEADS, HEADDIM)
    new_state_ssm = jnp.transpose(ns, (0, 1, 2, 4, 5, 3))
    return (y_prompt, y_sample, new_state_ssm)
```

```python
import functools

import numpy as np
import jax
import jax.numpy as jnp
from jax import lax
from jax.experimental import pallas as pl
from jax.experimental.pallas import tpu as pltpu

F32 = jnp.float32
BF16 = jnp.bfloat16
I32 = jnp.int32

D_MODEL = 1024
HEADDIM = 64
N_HEADS = 16
N_BC_GROUPS = 2
HEADS_PER_BC = N_HEADS // N_BC_GROUPS
D_STATE = 128
CHUNK = 128
D_SSD = N_HEADS * HEADDIM
D_BC = N_BC_GROUPS * D_STATE
CONV_DIM = D_SSD + 2 * D_BC
POOL_WINDOWS = (2, 4, 8, 16)
POOL_GROUP_W = 256
GRID_W = 64
N_EXPERTS = 16
EXPERTS_PER_GROUP = 4
N_EXPERT_GROUPS = N_EXPERTS // EXPERTS_PER_GROUP
D_FF = 512
EPS = 1e-6

LANES = 128
HEAD_QUAD = 4
QUAD_W = HEAD_QUAD * HEADDIM
EXPERT_PAIRS = ((0, 1), (0, 2), (0, 3), (1, 2), (1, 3), (2, 3))
N_CLASSES = N_EXPERT_GROUPS * len(EXPERT_PAIRS)

TM_PROJ = 512
TM_TOK = 256
TM_EXP = 256
GATE_LANES = 128
ROW_W = D_MODEL + GATE_LANES
DMA_UNROLL = 8
NEG_INF = float("-inf")


def _dot(a, b):
    return jnp.dot(a, b, preferred_element_type=F32)


def _split_bf16(x):
    hi = x.astype(BF16)
    lo = (x - hi.astype(F32)).astype(BF16)
    return hi, lo


def _dot_split(a, w):
    ah, al = _split_bf16(a)
    wh, wl = _split_bf16(w)
    return _dot(ah, wh) + (_dot(al, wh) + _dot(ah, wl))


def _silu(x):
    return x / (1.0 + jnp.exp(-x))


def _rms(x):
    return x * lax.rsqrt(jnp.mean(x * x, axis=-1, keepdims=True) + EPS)


def _ada_kernel(c_ref, w_ref, b_ref, o_ref):
    o_ref[0] = _dot_split(_silu(c_ref[...]), w_ref[0]) + b_ref[0]


def _ada_call(cvec, w_ada, b_ada):
    depth, d, n = w_ada.shape
    tn = 1024
    return pl.pallas_call(
        _ada_kernel,
        grid=(depth, n // tn),
        in_specs=[
            pl.BlockSpec((8, d), lambda l, j: (0, 0)),
            pl.BlockSpec((1, d, tn), lambda l, j: (l, 0, j)),
            pl.BlockSpec((1, 1, tn), lambda l, j: (l, 0, j)),
        ],
        out_specs=pl.BlockSpec((1, 8, tn), lambda l, j: (l, 0, j)),
        out_shape=jax.ShapeDtypeStruct((depth, 8, n), F32),
        name="ada_ln",
    )(cvec, w_ada, b_ada.reshape(depth, 1, n))


def _group_specs(tm, width, p_rows, s_rows):
    pt, st = p_rows // tm, s_rows // tm
    return [pl.BlockSpec((tm, width), lambda i, *_: (jnp.minimum(i, pt - 1), 0)),
            pl.BlockSpec((tm, width), lambda i, *_: (jnp.clip(i - pt, 0, st - 1), 0))]


def _inproj_kernel(xp_ref, xs_ref, mod_ref, nw_ref, wz_ref, wx_ref, wv_ref, wd_ref,
                   z_ref, xbc_ref, v_ref, dt_ref, *, prompt_tiles):
    x = jnp.where(pl.program_id(0) < prompt_tiles, xp_ref[...], xs_ref[...])
    h = _rms(x) * nw_ref[...]
    h = h * (1.0 + mod_ref[0, 1:2, :]) + mod_ref[0, 0:1, :]
    hb = h.astype(BF16)
    z_ref[...] = _dot(hb, wz_ref[...]).astype(BF16)
    xbc_ref[...] = _dot(hb, wx_ref[...])
    v_ref[...] = _dot(hb, wv_ref[...]).astype(BF16)
    dt_ref[...] = _dot(hb, wd_ref[...])


def _inproj_call(xp, xs, mod, nw, wz, wx, wv, wd, mod_row):
    t = xp.shape[0] + xs.shape[0]
    tm = TM_PROJ
    const = lambda i: (0, 0)
    row = lambda i: (i, 0)
    return pl.pallas_call(
        functools.partial(_inproj_kernel, prompt_tiles=xp.shape[0] // tm),
        grid=(t // tm,),
        in_specs=_group_specs(tm, D_MODEL, xp.shape[0], xs.shape[0]) + [
            pl.BlockSpec((1, 8, D_MODEL), lambda i: (mod_row(i * tm), 0, 0)),
            pl.BlockSpec((1, D_MODEL), const),
            pl.BlockSpec(wz.shape, const),
            pl.BlockSpec(wx.shape, const),
            pl.BlockSpec(wv.shape, const),
            pl.BlockSpec(wd.shape, const),
        ],
        out_specs=[
            pl.BlockSpec((tm, D_SSD), row),
            pl.BlockSpec((tm, CONV_DIM), row),
            pl.BlockSpec((tm, D_MODEL), row),
            pl.BlockSpec((tm, LANES), row),
        ],
        out_shape=[
            jax.ShapeDtypeStruct((t, D_SSD), BF16),
            jax.ShapeDtypeStruct((t, CONV_DIM), F32),
            jax.ShapeDtypeStruct((t, D_MODEL), BF16),
            jax.ShapeDtypeStruct((t, LANES), F32),
        ],
        name="in_proj",
    )(xp, xs, mod, nw, wz, wx, wv, wd)


def _expand_heads(val, e_ref):
    hi, lo = _split_bf16(val)
    return _dot(hi, e_ref[...]) + _dot(lo, e_ref[...])


def _ssd_kernel(*refs, nc, has_init, has_final):
    it = iter(refs)
    xm_ref, xp_ref, xn_ref, dt_ref, z_ref = (next(it) for _ in range(5))
    init_ref = next(it) if has_init else None
    cw_ref, cb_ref, dtb_ref, alog_ref, dsk_ref, nw_ref, e_ref = (next(it) for _ in range(7))
    y_ref = next(it)
    fin_ref = next(it) if has_final else None
    st_ref, yb_ref = next(it), next(it)

    ph = pl.program_id(1)
    c = pl.program_id(2)
    fwd = ph == 1
    cc = jnp.where(fwd, c, nc - 1 - c)
    q = CHUNK

    @pl.when(c == 0)
    def _():
        if has_init:
            st_ref[...] = init_ref[0, 0, 0].T
        else:
            st_ref[...] = jnp.zeros_like(st_ref)

    xm = xm_ref[...]
    prev_row = jnp.where(cc > 0, xp_ref[7:8, :], 0.0)
    next0 = jnp.where(cc < nc - 1, xn_ref[0:1, :], 0.0)
    next1 = jnp.where(cc < nc - 1, xn_ref[1:2, :], 0.0)
    rows = lax.broadcasted_iota(I32, (q, 1), 0)
    x_m1 = jnp.where(rows == 0, prev_row, pltpu.roll(xm, 1, 0))
    x_p1 = jnp.where(rows == q - 1, next0, pltpu.roll(xm, q - 1, 0))
    x_p2 = jnp.where(rows == q - 2, next0, jnp.where(rows == q - 1, next1, pltpu.roll(xm, q - 2, 0)))
    conv = (cw_ref[0:1, :] * x_m1 + cw_ref[1:2, :] * xm + cw_ref[2:3, :] * x_p1
            + cw_ref[3:4, :] * x_p2 + cb_ref[...])
    act = _silu(conv)
    xs = act[:, :D_SSD]
    xs_b = xs.astype(BF16)
    b_parts = [act[:, D_SSD + g * D_STATE:D_SSD + (g + 1) * D_STATE] for g in range(N_BC_GROUPS)]
    c_parts = [act[:, D_SSD + D_BC + g * D_STATE:D_SSD + D_BC + (g + 1) * D_STATE].astype(BF16)
               for g in range(N_BC_GROUPS)]
    bt_parts = [b.T for b in b_parts]
    g_parts = [lax.dot_general(c_parts[g], b_parts[g].astype(BF16), (((1,), (1,)), ((), ())),
                               preferred_element_type=F32) for g in range(N_BC_GROUPS)]

    dt_t = dt_ref[...].T
    dsel = jnp.where(fwd, dt_t[0:N_HEADS, :], dt_t[N_HEADS:2 * N_HEADS, :])
    dtb = jnp.where(fwd, dtb_ref[0:N_HEADS, :], dtb_ref[N_HEADS:2 * N_HEADS, :])
    alog = jnp.where(fwd, alog_ref[0:N_HEADS, :], alog_ref[N_HEADS:2 * N_HEADS, :])
    xdt = dsel + dtb
    dtv = jnp.maximum(xdt, 0.0) + jnp.log(1.0 + jnp.exp(-jnp.abs(xdt)))
    a_step = dtv * (-jnp.exp(alog))
    lane = lax.broadcasted_iota(I32, (N_HEADS, q), 1)
    pref = a_step
    k = 1
    while k < q:
        pref = pref + jnp.where(lane >= k, pltpu.roll(pref, k, 1), 0.0)
        k *= 2
    total = pref[:, q - 1:q]
    acum = jnp.where(fwd, pref, total - pref + a_step)
    w_t = dtv * jnp.exp(total - acum)
    e_acum = jnp.exp(acum)
    cols = jnp.concatenate([acum, e_acum, jnp.zeros((q - 2 * N_HEADS, q), F32)], axis=0).T
    acum_c = cols[:, 0:N_HEADS]
    e_acum_c = cols[:, N_HEADS:2 * N_HEADS]

    li = lax.broadcasted_iota(I32, (q, q), 0)
    si = lax.broadcasted_iota(I32, (q, q), 1)
    dls = li - si
    causal = jnp.where(fwd, dls, -dls) >= 0

    lane_q = lax.broadcasted_iota(I32, (q, QUAD_W), 1) // HEADDIM
    y_parts, s_parts = [], []
    for qd in range(N_HEADS // HEAD_QUAD):
        top, bot = [], []
        for j in range(HEAD_QUAD):
            h = qd * HEAD_QUAD + j
            g = h // HEADS_PER_BC
            seg = acum_c[:, h:h + 1] - acum[h:h + 1, :]
            dec = jnp.where(causal, jnp.exp(seg), 0.0)
            top.append((g_parts[g] * dec * dtv[h:h + 1, :]).astype(BF16))
            bot.append((bt_parts[g] * w_t[h:h + 1, :]).astype(BF16))
        lhs = jnp.concatenate([jnp.concatenate(top, axis=1), jnp.concatenate(bot, axis=1)], axis=0)
        xq = xs_b[:, qd * QUAD_W:(qd + 1) * QUAD_W]
        rhs = jnp.concatenate([jnp.where(lane_q == j, xq, jnp.zeros_like(xq)) for j in range(HEAD_QUAD)], axis=0)
        res = _dot(lhs, rhs)
        y_parts.append(res[:q])
        s_parts.append(res[q:])
    y_diag = jnp.concatenate(y_parts, axis=1)
    st_new = jnp.concatenate(s_parts, axis=1)

    st = st_ref[...]
    st_b = st.astype(BF16)
    half = D_SSD // N_BC_GROUPS
    y_off = jnp.concatenate([_dot(c_parts[g], st_b[:, g * half:(g + 1) * half]) for g in range(N_BC_GROUPS)],
                            axis=1)
    y_dir = y_diag + y_off * _expand_heads(e_acum_c, e_ref)
    e_tot = jnp.sum(jnp.exp(total) * e_ref[...].astype(F32), axis=0, keepdims=True)
    st_next = st * e_tot + st_new
    st_ref[...] = st_next

    if has_final:
        @pl.when(c == nc - 1)
        def _():
            fin_ref[0, 0] = st_next.T

    r0 = pl.multiple_of(cc * q, q)

    @pl.when(ph == 0)
    def _():
        yb_ref[pl.ds(r0, q), :] = y_dir

    @pl.when(ph == 1)
    def _():
        y = y_dir + yb_ref[pl.ds(r0, q), :] + xs * (dsk_ref[0:1, :] + dsk_ref[1:2, :])
        y = y * _silu(z_ref[...].astype(F32))
        y_ref[...] = (_rms(y) * nw_ref[...]).astype(BF16)


def _ssd_call(xbc, dt, z, init, layer, cw, cb, dtb, alog, dsk, nw, e_mat, *, n_seq, nc, base, has_final):
    t = xbc.shape[0]
    q = CHUNK
    has_init = init is not None
    last8 = t // 8 - 1

    def chunk_of(s, ph, c):
        return base + s * nc + jnp.where(ph == 1, c, nc - 1 - c)

    main = lambda s, ph, c: (chunk_of(s, ph, c), 0)
    prev = lambda s, ph, c: (jnp.maximum(chunk_of(s, ph, c) * (q // 8) - 1, 0), 0)
    nxt = lambda s, ph, c: (jnp.minimum((chunk_of(s, ph, c) + 1) * (q // 8), last8), 0)
    const = lambda s, ph, c: (0, 0)

    in_specs = [
        pl.BlockSpec((q, CONV_DIM), main),
        pl.BlockSpec((8, CONV_DIM), prev),
        pl.BlockSpec((8, CONV_DIM), nxt),
        pl.BlockSpec((q, LANES), main),
        pl.BlockSpec((q, D_SSD), lambda s, ph, c: (base + s * nc + ph * c, 0)),
    ]
    args = [xbc, xbc, xbc, dt, z]
    if has_init:
        in_specs.append(pl.BlockSpec((1, 1, 1, D_SSD, D_STATE), lambda s, ph, c: (s, layer, 1 - ph, 0, 0)))
        args.append(init)
    for a in (cw, cb, dtb, alog, dsk, nw, e_mat):
        in_specs.append(pl.BlockSpec(a.shape, const))
        args.append(a)

    out_specs = [pl.BlockSpec((q, D_SSD), lambda s, ph, c: (s * nc + ph * c, 0))]
    out_shape = [jax.ShapeDtypeStruct((n_seq * nc * q, D_SSD), BF16)]
    if has_final:
        out_specs.append(pl.BlockSpec((1, 1, D_SSD, D_STATE), lambda s, ph, c: (s, 1 - ph, 0, 0)))
        out_shape.append(jax.ShapeDtypeStruct((n_seq, 2, D_SSD, D_STATE), F32))

    out = pl.pallas_call(
        functools.partial(_ssd_kernel, nc=nc, has_init=has_init, has_final=has_final),
        grid=(n_seq, 2, nc),
        in_specs=in_specs,
        out_specs=out_specs,
        out_shape=out_shape,
        scratch_shapes=[pltpu.VMEM((D_STATE, D_SSD), F32), pltpu.VMEM((nc * q, D_SSD), F32)],
        compiler_params=pltpu.CompilerParams(dimension_semantics=("arbitrary", "arbitrary", "arbitrary")),
        name="ssd_mixer_nc%d" % nc,
    )(*args)
    return out if has_final else (out[0], None)


def _window_count(idx, w, n):
    hi = jnp.minimum(idx + (w // 2 - 1), n - 1)
    lo = jnp.maximum(idx - w // 2, 0)
    return hi - lo + 1


def _pool_tail(s, cnt, vt, g, pw_ref, ps_ref):
    cnt2 = jnp.concatenate([cnt, cnt], axis=1).astype(F32)
    upre = s / cnt2 - vt.astype(F32)
    gs = slice(g * POOL_GROUP_W, (g + 1) * POOL_GROUP_W)
    return (_dot(upre.astype(BF16), pw_ref[g]) * ps_ref[:, gs]).astype(BF16)


def _pool_seq_kernel(v_ref, band_ref, pw_ref, ps_ref, u_ref):
    n = v_ref.shape[0]
    pos = lax.broadcasted_iota(I32, (n, LANES), 0)
    for g, w in enumerate(POOL_WINDOWS):
        gs = slice(g * POOL_GROUP_W, (g + 1) * POOL_GROUP_W)
        vt = v_ref[:, gs]
        s = _dot(band_ref[g], vt)
        u_ref[:, gs] = _pool_tail(s, _window_count(pos, w, n), vt, g, pw_ref, ps_ref)


def _pool_grid_kernel(v_ref, band_ref, pw_ref, ps_ref, u_ref, p_ref, *, rows):
    tile = 256
    n = rows * GRID_W
    pad = (max(POOL_WINDOWS) // 2) * GRID_W
    p_ref[0:pad, :] = jnp.zeros((pad, POOL_GROUP_W), F32)
    p_ref[pad + n:pad + n + pad, :] = jnp.zeros((pad, POOL_GROUP_W), F32)
    for g, w in enumerate(POOL_WINDOWS):
        gs = slice(g * POOL_GROUP_W, (g + 1) * POOL_GROUP_W)

        def colsum(t, carry):
            r0 = pl.multiple_of(t * tile, tile)
            p_ref[pl.ds(pad + r0, tile), :] = _dot(band_ref[g], v_ref[pl.ds(r0, tile), gs])
            return carry

        lax.fori_loop(0, n // tile, colsum, 0)

        def rowsum(t, carry):
            r0 = pl.multiple_of(t * tile, tile)
            s = p_ref[pl.ds(pad + r0 - (w // 2) * GRID_W, tile), :]
            for d in range(-(w // 2) + 1, w // 2):
                s = s + p_ref[pl.ds(pad + r0 + d * GRID_W, tile), :]
            tok = r0 + lax.broadcasted_iota(I32, (tile, LANES), 0)
            grid_row = jnp.right_shift(tok, GRID_W.bit_length() - 1)
            grid_col = jnp.bitwise_and(tok, GRID_W - 1)
            cnt = _window_count(grid_row, w, rows) * _window_count(grid_col, w, GRID_W)
            vt = v_ref[pl.ds(r0, tile), gs]
            u_ref[pl.ds(r0, tile), gs] = _pool_tail(s, cnt, vt, g, pw_ref, ps_ref)
            return carry

        lax.fori_loop(0, n // tile, rowsum, 0)


def _band_matrix(n, w):
    idx = np.arange(n)
    lo = np.clip(idx - w // 2, 0, n - 1)
    hi = np.clip(idx + w // 2 - 1, 0, n - 1)
    j = idx[None, :]
    return ((j >= lo[:, None]) & (j <= hi[:, None])).astype(np.float32)


def _pool_call(v, pw, ps, *, n_seq, seq_len, base_block, grid_rows):
    if grid_rows is None:
        bands = np.stack([_band_matrix(seq_len, w) for w in POOL_WINDOWS])
        kern = _pool_seq_kernel
        scratch = []
    else:
        per = 256 // GRID_W
        bands = np.stack([np.kron(np.eye(per, dtype=np.float32), _band_matrix(GRID_W, w)) for w in POOL_WINDOWS])
        kern = functools.partial(_pool_grid_kernel, rows=grid_rows)
        pad = (max(POOL_WINDOWS) // 2) * GRID_W
        scratch = [pltpu.VMEM((seq_len + 2 * pad, POOL_GROUP_W), F32)]
    bands = jnp.asarray(bands, BF16)
    const3 = lambda s: (0, 0, 0)
    return pl.pallas_call(
        kern,
        grid=(n_seq,),
        in_specs=[
            pl.BlockSpec((seq_len, D_MODEL), lambda s: (base_block + s, 0)),
            pl.BlockSpec(bands.shape, const3),
            pl.BlockSpec(pw.shape, const3),
            pl.BlockSpec(ps.shape, lambda s: (0, 0)),
        ],
        out_specs=pl.BlockSpec((seq_len, D_MODEL), lambda s: (s, 0)),
        out_shape=jax.ShapeDtypeStruct((n_seq * seq_len, D_MODEL), BF16),
        scratch_shapes=scratch,
        name="pool_mixer_len%d" % seq_len,
    )(v, bands, pw, ps)


def _route_kernel(yp_ref, ys_ref, up_ref, us_ref, xp_ref, xs_ref, mod_ref, wy_ref, wu_ref, nw_ref, wr_ref, rb_ref,
                  x1_ref, h2_ref, cls_ref, *, prompt_tiles):
    tm = x1_ref.shape[0]
    is_prompt = pl.program_id(0) < prompt_tiles
    y = jnp.where(is_prompt, yp_ref[...], ys_ref[...])
    u = jnp.where(is_prompt, up_ref[...], us_ref[...])
    x = jnp.where(is_prompt, xp_ref[...], xs_ref[...])
    mix = _dot(y, wy_ref[0]) + _dot(u, wu_ref[0])
    x1 = x + mod_ref[0, 2:3, :] * mix
    x1_ref[...] = x1
    h2 = (_rms(x1) * nw_ref[...]) * (1.0 + mod_ref[0, 4:5, :]) + mod_ref[0, 3:4, :]
    h2_ref[:, 0:D_MODEL] = h2

    logits = _dot_split(h2, wr_ref[...]).T[0:N_EXPERTS, :]
    e = jnp.exp(logits - jnp.max(logits, axis=0, keepdims=True))
    probs = e / jnp.sum(e, axis=0, keepdims=True)
    sel = probs + rb_ref[...]
    s_rows = [sel[i:i + 1, :] for i in range(N_EXPERTS)]
    p_rows = [probs[i:i + 1, :] for i in range(N_EXPERTS)]

    scores = []
    for g in range(N_EXPERT_GROUPS):
        a, b, c, d = s_rows[4 * g:4 * g + 4]
        hi1, lo1 = jnp.maximum(a, b), jnp.minimum(a, b)
        hi2, lo2 = jnp.maximum(c, d), jnp.minimum(c, d)
        scores.append(jnp.maximum(hi1, hi2) + jnp.maximum(jnp.minimum(hi1, hi2), jnp.maximum(lo1, lo2)))
    best_g = jnp.zeros((1, tm), I32)
    best_s = scores[0]
    for g in range(1, N_EXPERT_GROUPS):
        better = scores[g] > best_s
        best_g = jnp.where(better, g, best_g)
        best_s = jnp.where(better, scores[g], best_s)

    def in_group(rows, j):
        out = rows[j]
        for g in range(1, N_EXPERT_GROUPS):
            out = jnp.where(best_g == g, rows[4 * g + j], out)
        return out

    sg = [in_group(s_rows, j) for j in range(EXPERTS_PER_GROUP)]
    pg = [in_group(p_rows, j) for j in range(EXPERTS_PER_GROUP)]
    i1 = jnp.zeros((1, tm), I32)
    v1 = sg[0]
    for j in range(1, EXPERTS_PER_GROUP):
        better = sg[j] > v1
        i1 = jnp.where(better, j, i1)
        v1 = jnp.where(better, sg[j], v1)
    i2 = jnp.zeros((1, tm), I32)
    v2 = jnp.where(i1 == 0, NEG_INF, sg[0])
    for j in range(1, EXPERTS_PER_GROUP):
        cand = jnp.where(i1 == j, NEG_INF, sg[j])
        better = cand > v2
        i2 = jnp.where(better, j, i2)
        v2 = jnp.where(better, cand, v2)

    def pick(idx):
        out = pg[0]
        for j in range(1, EXPERTS_PER_GROUP):
            out = jnp.where(idx == j, pg[j], out)
        return out

    p1, p2 = pick(i1), pick(i2)
    psum = p1 + p2
    g1, g2 = p1 / psum, p2 / psum
    lo = jnp.minimum(i1, i2)
    hi = jnp.maximum(i1, i2)
    g_lo = jnp.where(i1 < i2, g1, g2)
    g_hi = jnp.where(i1 < i2, g2, g1)
    pair = jnp.where(lo == 0, hi - 1, jnp.where(lo == 1, hi + 1, 5))
    cls_ref[0] = best_g * len(EXPERT_PAIRS) + pair

    r = lax.broadcasted_iota(I32, (GATE_LANES, tm), 0)
    gates_t = jnp.where(r == 0, g_lo, jnp.where(r == 1, g_hi, 0.0))
    h2_ref[:, D_MODEL:ROW_W] = gates_t.T


def _route_call(yp, ys, up, us, xp, xs, mod, w_out, layer, nw, wr, rb, mod_row):
    t_p, t_s = xp.shape[0], xs.shape[0]
    t = t_p + t_s
    tm = TM_TOK
    row = lambda i: (i, 0)
    const = lambda i: (0, 0)
    return pl.pallas_call(
        functools.partial(_route_kernel, prompt_tiles=t_p // tm),
        grid=(t // tm,),
        in_specs=_group_specs(tm, D_SSD, t_p, t_s) + _group_specs(tm, D_MODEL, t_p, t_s)
        + _group_specs(tm, D_MODEL, t_p, t_s) + [
            pl.BlockSpec((1, 8, D_MODEL), lambda i: (mod_row(i * tm), 0, 0)),
            pl.BlockSpec((1, D_SSD, D_MODEL), lambda i: (layer, 0, 0)),
            pl.BlockSpec((1, D_MODEL, D_MODEL), lambda i: (layer, D_SSD // D_MODEL, 0)),
            pl.BlockSpec((1, D_MODEL), const),
            pl.BlockSpec(wr.shape, const),
            pl.BlockSpec(rb.shape, const),
        ],
        out_specs=[
            pl.BlockSpec((tm, D_MODEL), row),
            pl.BlockSpec((tm, ROW_W), row),
            pl.BlockSpec((1, 1, tm), lambda i: (i, 0, 0)),
        ],
        out_shape=[
            jax.ShapeDtypeStruct((t, D_MODEL), F32),
            jax.ShapeDtypeStruct((t, ROW_W), F32),
            jax.ShapeDtypeStruct((t // tm, 1, tm), I32),
        ],
        name="out_proj_router",
    )(yp, ys, up, us, xp, xs, mod, w_out, w_out, nw, wr, rb)


def _sort_kernel(cls_ref, pos_ref, tile_ref):
    cls = cls_ref[...]
    nr = cls.shape[0]
    ci = lax.broadcasted_iota(I32, (LANES, LANES), 0)
    cj = lax.broadcasted_iota(I32, (LANES, LANES), 1)
    upper = jnp.where(ci <= cj, 1.0, 0.0).astype(BF16)
    ri = lax.broadcasted_iota(I32, (nr, nr), 0)
    rj = lax.broadcasted_iota(I32, (nr, nr), 1)
    lower = jnp.where(rj < ri, 1.0, 0.0).astype(BF16)
    tile_idx = lax.broadcasted_iota(I32, (8, LANES), 1).astype(F32)
    pos = jnp.zeros(cls.shape, F32)
    tile_cls = jnp.zeros((8, LANES), F32)
    tiles_done = jnp.zeros((1, 1), F32)
    for k in range(N_CLASSES):
        member = cls == k
        mf = jnp.where(member, 1.0, 0.0)
        within = _dot(mf.astype(BF16), upper)
        row_tot = jnp.broadcast_to(within[:, LANES - 1:LANES], cls.shape)
        before = _dot(lower, row_tot.astype(BF16))
        rank = within - mf + before
        count = jnp.sum(jnp.sum(mf, axis=1, keepdims=True), axis=0, keepdims=True)
        n_tiles = jnp.floor((count + (TM_EXP - 1)) * (1.0 / TM_EXP))
        pos = pos + jnp.where(member, tiles_done * TM_EXP + rank, 0.0)
        tiles_done = tiles_done + n_tiles
        tile_cls = tile_cls + jnp.where(tile_idx >= tiles_done, 1.0, 0.0)
    pos_ref[...] = pos.astype(I32)
    r = lax.broadcasted_iota(I32, (8, LANES), 0)
    tile_ref[...] = jnp.where(r == 1, tiles_done, tile_cls).astype(I32)


def _sort_call(cls):
    t = cls.size
    rows = -(-t // (LANES * LANES)) * LANES
    cls2 = jnp.pad(cls.reshape(t // LANES, LANES), ((0, rows - t // LANES), (0, 0)), constant_values=N_CLASSES)
    pos, tiles = pl.pallas_call(
        _sort_kernel,
        out_shape=[jax.ShapeDtypeStruct(cls2.shape, I32), jax.ShapeDtypeStruct((8, LANES), I32)],
        name="route_sort",
    )(cls2)
    return pos.reshape(-1)[:t], tiles


def _scatter_kernel(pos_ref, src_ref, dst_in, dst_ref, sem):
    del dst_in
    tm = src_ref.shape[0]
    base = pl.program_id(0) * tm

    def row_copy(r):
        return pltpu.make_async_copy(src_ref.at[pl.ds(r, 1), :], dst_ref.at[pl.ds(pos_ref[base + r], 1), :], sem)

    def body(r, carry):
        row_copy(r).start()
        return carry

    lax.fori_loop(0, tm, body, 0, unroll=DMA_UNROLL)

    def drain(r, carry):
        row_copy(r).wait()
        return carry

    lax.fori_loop(0, tm, drain, 0, unroll=DMA_UNROLL)


def _scatter_call(pos, src, n_rows):
    t, w = src.shape
    tm = TM_TOK
    dst = jnp.zeros((n_rows, w), src.dtype)
    return pl.pallas_call(
        _scatter_kernel,
        grid_spec=pltpu.PrefetchScalarGridSpec(
            num_scalar_prefetch=1,
            grid=(t // tm,),
            in_specs=[pl.BlockSpec((tm, w), lambda i, pos: (i, 0)),
                      pl.BlockSpec(memory_space=pl.ANY)],
            out_specs=pl.BlockSpec(memory_space=pl.ANY),
            scratch_shapes=[pltpu.SemaphoreType.DMA(())],
        ),
        out_shape=jax.ShapeDtypeStruct(dst.shape, dst.dtype),
        input_output_aliases={2: 0},
        name="route_scatter",
    )(pos, src, dst)


def _unpermute_kernel(pos_ref, x1_ref, mod_ref, nw_ref, y_hbm, op_ref, os_ref, buf, sem, *, final_norm,
                      prompt_tiles):
    tm = x1_ref.shape[0]
    base = pl.program_id(0) * tm

    def row_copy(r):
        return pltpu.make_async_copy(y_hbm.at[pl.ds(pos_ref[base + r], 1), :], buf.at[pl.ds(r, 1), :], sem)

    def body(r, carry):
        row_copy(r).start()
        return carry

    lax.fori_loop(0, tm, body, 0, unroll=DMA_UNROLL)

    def drain(r, carry):
        row_copy(r).wait()
        return carry

    lax.fori_loop(0, tm, drain, 0, unroll=DMA_UNROLL)
    x2 = x1_ref[...] + mod_ref[0, 5:6, :] * buf[...]
    if final_norm:
        x2 = _rms(x2) * nw_ref[...]

    @pl.when(pl.program_id(0) < prompt_tiles)
    def _():
        op_ref[...] = x2

    @pl.when(pl.program_id(0) >= prompt_tiles)
    def _():
        os_ref[...] = x2


def _unpermute_call(pos, x1, mod, nw, y_sorted, mod_row, t_p, final_norm):
    t = x1.shape[0]
    tm = TM_TOK
    return pl.pallas_call(
        functools.partial(_unpermute_kernel, final_norm=final_norm, prompt_tiles=t_p // tm),
        grid_spec=pltpu.PrefetchScalarGridSpec(
            num_scalar_prefetch=1,
            grid=(t // tm,),
            in_specs=[pl.BlockSpec((tm, D_MODEL), lambda i, pos: (i, 0)),
                      pl.BlockSpec((1, 8, D_MODEL), lambda i, pos: (mod_row(i * tm), 0, 0)),
                      pl.BlockSpec((1, D_MODEL), lambda i, pos: (0, 0)),
                      pl.BlockSpec(memory_space=pl.ANY)],
            out_specs=_group_specs(tm, D_MODEL, t_p, t - t_p),
            scratch_shapes=[pltpu.VMEM((tm, D_MODEL), F32), pltpu.SemaphoreType.DMA(())],
        ),
        out_shape=[jax.ShapeDtypeStruct((t_p, D_MODEL), F32), jax.ShapeDtypeStruct((t - t_p, D_MODEL), F32)],
        name="moe_unpermute",
    )(pos, x1, mod, nw, y_sorted)


def _moe_kernel(ea_ref, eb_ref, nact_ref, x_ref, wga, wua, wda, wgb, wub, wdb, y_ref):
    del ea_ref, eb_ref
    j = pl.program_id(0)

    @pl.when(j < nact_ref[0])
    def _():
        x = x_ref[:, 0:D_MODEL].astype(BF16)
        gate_a = x_ref[:, D_MODEL:D_MODEL + 1]
        gate_b = x_ref[:, D_MODEL + 1:D_MODEL + 2]

        def ffn(wg, wu, wd, gate):
            g = _dot(x, wg[0, 0].astype(BF16))
            u = _dot(x, wu[0, 0].astype(BF16))
            a = _silu(g) * u * gate
            return _dot(a.astype(BF16), wd[0, 0].astype(BF16))

        y_ref[...] = ffn(wga, wua, wda, gate_a) + ffn(wgb, wub, wdb, gate_b)

    @pl.when(j >= nact_ref[0])
    def _():
        y_ref[...] = jnp.zeros_like(y_ref)


def _moe_call(ea, eb, nact, xs, w_gate, w_up, w_down, layer):
    n_rows = xs.shape[0]
    tm = TM_EXP
    wa = lambda j, ea, eb, na: (layer, ea[j], 0, 0)
    wb = lambda j, ea, eb, na: (layer, eb[j], 0, 0)
    gu = (1, 1, D_MODEL, D_FF)
    dn = (1, 1, D_FF, D_MODEL)
    return pl.pallas_call(
        _moe_kernel,
        grid_spec=pltpu.PrefetchScalarGridSpec(
            num_scalar_prefetch=3,
            grid=(n_rows // tm,),
            in_specs=[pl.BlockSpec((tm, ROW_W), lambda j, ea, eb, na: (j, 0)),
                      pl.BlockSpec(gu, wa), pl.BlockSpec(gu, wa), pl.BlockSpec(dn, wa),
                      pl.BlockSpec(gu, wb), pl.BlockSpec(gu, wb), pl.BlockSpec(dn, wb)],
            out_specs=pl.BlockSpec((tm, D_MODEL), lambda j, ea, eb, na: (j, 0)),
        ),
        out_shape=jax.ShapeDtypeStruct((n_rows, D_MODEL), F32),
        name="moe_ffn",
    )(ea, eb, nact, xs, w_gate, w_up, w_down, w_gate, w_up, w_down)


def kernel(x_prompt, x_sample, state_ssm, c, c_ctx, w_ada, b_ada, norm1_w, w_in, conv_w, conv_b, dt_bias, a_log, d_skip, ssd_norm_w, pool_w, pool_scale, w_out, norm2_w, w_router, router_bias, w_gate, w_up, w_down, final_norm_w):
    n_p, len_p, d = x_prompt.shape
    n_s, len_s, _ = x_sample.shape
    depth = w_ada.shape[0]
    t_p, t_s = n_p * len_p, n_s * len_s
    t = t_p + t_s
    assert d == D_MODEL and len_p % TM_TOK == 0 and len_s % TM_PROJ == 0 and t_p % TM_PROJ == 0
    assert len_s % GRID_W == 0 and t_p % len_s == 0

    def mod_row(tok):
        return jnp.where(tok < t_p, 0, 1 + (tok - t_p) // len_s)

    xp, xs = x_prompt.reshape(t_p, d), x_sample.reshape(t_s, d)

    cvec = jnp.zeros((8, d), F32).at[0].set(c_ctx).at[1:1 + n_s].set(c)
    ada = _ada_call(cvec, w_ada, b_ada)
    mods = jnp.pad(ada[:, :1 + n_s].reshape(depth, 1 + n_s, 6, d), ((0, 0), (0, 0), (0, 2), (0, 0)))

    e_mat = jnp.asarray(np.kron(np.eye(N_HEADS, dtype=np.float32), np.ones((1, HEADDIM), np.float32)), BF16)
    wr = jnp.pad(w_router, ((0, 0), (0, LANES - N_EXPERTS)))
    rb = jnp.broadcast_to(router_bias[:, None], (N_EXPERTS, TM_TOK))
    pairs = np.asarray(EXPERT_PAIRS, np.int32)
    n_sorted = t + N_CLASSES * TM_EXP
    n_tiles = n_sorted // TM_EXP
    assert n_tiles <= LANES

    init_all = state_ssm.reshape(n_s, depth, 2, D_SSD, D_STATE)
    w_out_b = w_out.astype(BF16)

    new_states = []
    for l in range(depth):
        w_l = w_in[l]
        wz = w_l[:, :D_SSD].astype(BF16)
        wx = w_l[:, D_SSD:D_SSD + CONV_DIM].astype(BF16)
        o = D_SSD + CONV_DIM
        wd = jnp.pad(w_l[:, o:o + 2 * N_HEADS], ((0, 0), (0, LANES - 2 * N_HEADS))).astype(BF16)
        wv = w_l[:, o + 2 * N_HEADS:].astype(BF16)
        z, xbc, v, dt = _inproj_call(xp, xs, mods[l], norm1_w[l][None], wz, wx, wv, wd, mod_row)

        dtb = jnp.broadcast_to(dt_bias[l].reshape(2 * N_HEADS, 1), (2 * N_HEADS, CHUNK))
        alog = jnp.broadcast_to(a_log[l].reshape(2 * N_HEADS, 1), (2 * N_HEADS, CHUNK))
        dsk = jnp.repeat(d_skip[l], HEADDIM, axis=1)
        ssd_args = (conv_w[l], conv_b[l][None], dtb, alog, dsk, ssd_norm_w[l][None], e_mat)
        y_p, fin = _ssd_call(xbc, dt, z, None, l, *ssd_args, n_seq=n_p, nc=len_p // CHUNK, base=0, has_final=True)
        y_s, _ = _ssd_call(xbc, dt, z, init_all, l, *ssd_args, n_seq=n_s, nc=len_s // CHUNK,
                           base=t_p // CHUNK, has_final=False)
        new_states.append(fin)

        pw = pool_w[l].astype(BF16)
        ps = pool_scale[l][None]
        u_p = _pool_call(v, pw, ps, n_seq=n_p, seq_len=len_p, base_block=0, grid_rows=None)
        u_s = _pool_call(v, pw, ps, n_seq=n_s, seq_len=len_s, base_block=t_p // len_s, grid_rows=len_s // GRID_W)

        x1, h2, cls = _route_call(y_p, y_s, u_p, u_s, xp, xs, mods[l], w_out_b, l, norm2_w[l][None], wr, rb, mod_row)

        pos, tiles = _sort_call(cls)
        nact = tiles[1, 0:1]
        tcls = tiles[0, :n_tiles]
        tcls = jnp.minimum(tcls, tcls[jnp.maximum(nact[0] - 1, 0)])
        tcls = jnp.minimum(tcls, N_CLASSES - 1)
        grp = tcls // len(EXPERT_PAIRS)
        pr = tcls % len(EXPERT_PAIRS)
        ea = grp * EXPERTS_PER_GROUP + jnp.asarray(pairs[:, 0])[pr]
        eb = grp * EXPERTS_PER_GROUP + jnp.asarray(pairs[:, 1])[pr]

        h2s = _scatter_call(pos, h2, n_sorted)
        y_sorted = _moe_call(ea, eb, nact, h2s, w_gate, w_up, w_down, l)
        last = l == depth - 1
        xp, xs = _unpermute_call(pos, x1, mods[l], final_norm_w[None], y_sorted, mod_row, t_p, last)

    y_prompt = xp.reshape(n_p, len_p, d)
    y_sample = xs.reshape(n_s, len_s, d)
    ns = jnp.stack(new_states, axis=1)
    new_state_ssm = ns.reshape(n_p, depth, 2, N_HEADS, HEADDIM, D_STATE)
    return (y_prompt, y_sample, new_state_ssm)
```

```python
import functools

import numpy as np
import jax
import jax.numpy as jnp
from jax import lax
from jax.experimental import pallas as pl
from jax.experimental.pallas import tpu as pltpu

F32 = jnp.float32
BF16 = jnp.bfloat16
I32 = jnp.int32

D_MODEL = 1024
HEADDIM = 64
N_HEADS = 16
N_BC_GROUPS = 2
HEADS_PER_BC = N_HEADS // N_BC_GROUPS
D_STATE = 128
CHUNK = 128
D_SSD = N_HEADS * HEADDIM
D_BC = N_BC_GROUPS * D_STATE
CONV_DIM = D_SSD + 2 * D_BC
POOL_WINDOWS = (2, 4, 8, 16)
POOL_GROUP_W = 256
GRID_W = 64
N_EXPERTS = 16
EXPERTS_PER_GROUP = 4
N_EXPERT_GROUPS = N_EXPERTS // EXPERTS_PER_GROUP
D_FF = 512
EPS = 1e-6

LANES = 128
HEAD_QUAD = 4
QUAD_W = HEAD_QUAD * HEADDIM
EXPERT_PAIRS = ((0, 1), (0, 2), (0, 3), (1, 2), (1, 3), (2, 3))
N_CLASSES = N_EXPERT_GROUPS * len(EXPERT_PAIRS)

TM_PROJ = 512
TM_TOK = 256
TM_EXP = 256
SLAB = D_MODEL // LANES
DMA_UNROLL = 8
NEG_INF = float("-inf")


def _dot(a, b):
    return jnp.dot(a, b, preferred_element_type=F32)


def _split_bf16(x):
    hi = x.astype(BF16)
    lo = (x - hi.astype(F32)).astype(BF16)
    return hi, lo


def _dot_split(a, w):
    ah, al = _split_bf16(a)
    wh, wl = _split_bf16(w)
    return _dot(ah, wh) + (_dot(al, wh) + _dot(ah, wl))


def _silu(x):
    return x / (1.0 + jnp.exp(-x))


def _rms(x):
    return x * lax.rsqrt(jnp.mean(x * x, axis=-1, keepdims=True) + EPS)


def _store_slabs(ref, val):
    n = val.shape[0]
    for k in range(SLAB):
        ref[pl.ds(k, n, stride=SLAB), :] = val[:, k * LANES:(k + 1) * LANES]


def _load_slabs(ref, n):
    return jnp.concatenate([ref[pl.ds(k, n, stride=SLAB), :] for k in range(SLAB)], axis=1)


def _ada_kernel(c_ref, w_ref, b_ref, o_ref):
    o_ref[0] = _dot_split(_silu(c_ref[...]), w_ref[0]) + b_ref[0]


def _ada_call(cvec, w_ada, b_ada):
    depth, d, n = w_ada.shape
    tn = 1024
    return pl.pallas_call(
        _ada_kernel,
        grid=(depth, n // tn),
        in_specs=[
            pl.BlockSpec((8, d), lambda l, j: (0, 0)),
            pl.BlockSpec((1, d, tn), lambda l, j: (l, 0, j)),
            pl.BlockSpec((1, 1, tn), lambda l, j: (l, 0, j)),
        ],
        out_specs=pl.BlockSpec((1, 8, tn), lambda l, j: (l, 0, j)),
        out_shape=jax.ShapeDtypeStruct((depth, 8, n), F32),
        name="ada_ln",
    )(cvec, w_ada, b_ada.reshape(depth, 1, n))


def _group_specs(tm, width, p_rows, s_rows):
    pt, st = p_rows // tm, s_rows // tm
    return [pl.BlockSpec((tm, width), lambda i, *_: (jnp.minimum(i, pt - 1), 0)),
            pl.BlockSpec((tm, width), lambda i, *_: (jnp.clip(i - pt, 0, st - 1), 0))]


def _inproj_kernel(xp_ref, xs_ref, mod_ref, nw_ref, wz_ref, wx_ref, wv_ref, wd_ref,
                   z_ref, xbc_ref, v_ref, dt_ref, *, prompt_tiles):
    x = jnp.where(pl.program_id(0) < prompt_tiles, xp_ref[...], xs_ref[...])
    h = _rms(x) * nw_ref[...]
    h = h * (1.0 + mod_ref[0, 1:2, :]) + mod_ref[0, 0:1, :]
    hb = h.astype(BF16)
    z_ref[...] = _dot(hb, wz_ref[...]).astype(BF16)
    xbc_ref[...] = _dot(hb, wx_ref[...])
    v_ref[...] = _dot(hb, wv_ref[...]).astype(BF16)
    dt_ref[...] = _dot(hb, wd_ref[...])


def _inproj_call(xp, xs, mod, nw, wz, wx, wv, wd, mod_row):
    t = xp.shape[0] + xs.shape[0]
    tm = TM_PROJ
    const = lambda i: (0, 0)
    row = lambda i: (i, 0)
    return pl.pallas_call(
        functools.partial(_inproj_kernel, prompt_tiles=xp.shape[0] // tm),
        grid=(t // tm,),
        in_specs=_group_specs(tm, D_MODEL, xp.shape[0], xs.shape[0]) + [
            pl.BlockSpec((1, 8, D_MODEL), lambda i: (mod_row(i * tm), 0, 0)),
            pl.BlockSpec((1, D_MODEL), const),
            pl.BlockSpec(wz.shape, const),
            pl.BlockSpec(wx.shape, const),
            pl.BlockSpec(wv.shape, const),
            pl.BlockSpec(wd.shape, const),
        ],
        out_specs=[
            pl.BlockSpec((tm, D_SSD), row),
            pl.BlockSpec((tm, CONV_DIM), row),
            pl.BlockSpec((tm, D_MODEL), row),
            pl.BlockSpec((tm, LANES), row),
        ],
        out_shape=[
            jax.ShapeDtypeStruct((t, D_SSD), BF16),
            jax.ShapeDtypeStruct((t, CONV_DIM), F32),
            jax.ShapeDtypeStruct((t, D_MODEL), BF16),
            jax.ShapeDtypeStruct((t, LANES), F32),
        ],
        name="in_proj",
    )(xp, xs, mod, nw, wz, wx, wv, wd)


LOG2E = 1.4426950408889634
PREP_CHUNKS = 8


def _decay_kernel(dt_ref, dtb_ref, alog_ref, ra_ref, rw_ref, cc_ref):
    q = CHUNK
    ci = lax.broadcasted_iota(I32, (q, q), 0)
    cj = lax.broadcasted_iota(I32, (q, q), 1)
    upper = jnp.where(ci <= cj, 1.0, 0.0).astype(BF16)
    t_parts = [dt_ref[k * q:(k + 1) * q, :].T for k in range(PREP_CHUNKS)]
    for d in range(2):
        x = jnp.concatenate([t[d * N_HEADS:(d + 1) * N_HEADS, :] for t in t_parts], axis=0)
        tile = lambda r: jnp.concatenate([r[d * N_HEADS:(d + 1) * N_HEADS, :]] * PREP_CHUNKS, axis=0)
        xdt = x + tile(dtb_ref)
        dtv = jnp.maximum(xdt, 0.0) + jnp.log(1.0 + jnp.exp(-jnp.abs(xdt)))
        a_step = dtv * (-jnp.exp(tile(alog_ref)))
        hi = a_step.astype(BF16)
        r1 = a_step - hi.astype(F32)
        mid = r1.astype(BF16)
        lo = (r1 - mid.astype(F32)).astype(BF16)
        pref = _dot(hi, upper) + (_dot(mid, upper) + _dot(lo, upper))
        total = pref[:, q - 1:q]
        cum = pref if d == 0 else total - pref + a_step
        ra_ref[d] = (cum - jnp.log(dtv)) * LOG2E
        rw_ref[d] = dtv * jnp.exp(total - cum)
        cum2 = cum * LOG2E
        e_cum = jnp.exp(cum)
        pad = jnp.zeros((q - 2 * N_HEADS, q), F32)
        for k in range(PREP_CHUNKS):
            rs = slice(k * N_HEADS, (k + 1) * N_HEADS)
            cc_ref[d, k * q:(k + 1) * q, :] = jnp.concatenate([cum2[rs], e_cum[rs], pad], axis=0).T


def _decay_call(dt, dtb, alog):
    t = dt.shape[0]
    q = CHUNK
    tm = PREP_CHUNKS * q
    rows = PREP_CHUNKS * N_HEADS
    n_rows = t // q * N_HEADS
    const = lambda i: (0, 0)
    return pl.pallas_call(
        _decay_kernel,
        grid=(t // tm,),
        in_specs=[pl.BlockSpec((tm, LANES), lambda i: (i, 0)),
                  pl.BlockSpec(dtb.shape, const),
                  pl.BlockSpec(alog.shape, const)],
        out_specs=[pl.BlockSpec((2, rows, q), lambda i: (0, i, 0)),
                   pl.BlockSpec((2, rows, q), lambda i: (0, i, 0)),
                   pl.BlockSpec((2, tm, LANES), lambda i: (0, i, 0))],
        out_shape=[jax.ShapeDtypeStruct((2, n_rows, q), F32),
                   jax.ShapeDtypeStruct((2, n_rows, q), F32),
                   jax.ShapeDtypeStruct((2, t, LANES), F32)],
        name="ssd_decay_tables",
    )(dt, dtb, alog)


EXP2_CLAMP = 100.0


def _ssd_kernel(*refs, nc, has_init, has_final):
    it = iter(refs)
    xm_ref, xp_ref, xn_ref, ra_ref, rw_ref, cc_ref, z_ref = (next(it) for _ in range(7))
    init_ref = next(it) if has_init else None
    cw_ref, cb_ref, dsk_ref, nw_ref, e_ref = (next(it) for _ in range(5))
    y_ref = next(it)
    fin_ref = next(it) if has_final else None
    st_ref, yb_ref, xs_c, c_c, bt_c, g_c = (next(it) for _ in range(6))

    ph = pl.program_id(1)
    c = pl.program_id(2)
    fwd = ph == 1
    cc = jnp.where(fwd, c, nc - 1 - c)
    q = CHUNK
    r0 = pl.multiple_of(cc * q, q)
    r0g = pl.multiple_of(cc * (N_BC_GROUPS * q), N_BC_GROUPS * q)

    @pl.when(c == 0)
    def _():
        if has_init:
            st_ref[...] = init_ref[0, 0, 0].T
        else:
            st_ref[...] = jnp.zeros_like(st_ref)

    @pl.when(ph == 0)
    def _():
        xm = xm_ref[...]
        prev_row = jnp.where(cc > 0, xp_ref[7:8, :], 0.0)
        next0 = jnp.where(cc < nc - 1, xn_ref[0:1, :], 0.0)
        next1 = jnp.where(cc < nc - 1, xn_ref[1:2, :], 0.0)
        rows = lax.broadcasted_iota(I32, (q, 1), 0)
        x_m1 = jnp.where(rows == 0, prev_row, pltpu.roll(xm, 1, 0))
        x_p1 = jnp.where(rows == q - 1, next0, pltpu.roll(xm, q - 1, 0))
        x_p2 = jnp.where(rows == q - 2, next0, jnp.where(rows == q - 1, next1, pltpu.roll(xm, q - 2, 0)))
        conv = (cw_ref[0:1, :] * x_m1 + cw_ref[1:2, :] * xm + cw_ref[2:3, :] * x_p1
                + cw_ref[3:4, :] * x_p2 + cb_ref[...])
        act = _silu(conv)
        xs_c[pl.ds(r0, q), :] = act[:, :D_SSD].astype(BF16)
        c_c[pl.ds(r0, q), :] = act[:, D_SSD + D_BC:].astype(BF16)
        for g in range(N_BC_GROUPS):
            b_g = act[:, D_SSD + g * D_STATE:D_SSD + (g + 1) * D_STATE]
            c_g = act[:, D_SSD + D_BC + g * D_STATE:D_SSD + D_BC + (g + 1) * D_STATE].astype(BF16)
            bt_c[pl.ds(r0g + g * q, q), :] = b_g.T.astype(BF16)
            g_c[pl.ds(r0g + g * q, q), :] = lax.dot_general(
                c_g, b_g.astype(BF16), (((1,), (1,)), ((), ())), preferred_element_type=F32).astype(BF16)

    xs_b = xs_c[pl.ds(r0, q), :]
    c_all = c_c[pl.ds(r0, q), :]
    c_parts = [c_all[:, g * D_STATE:(g + 1) * D_STATE] for g in range(N_BC_GROUPS)]
    bt_parts = [bt_c[pl.ds(r0g + g * q, q), :] for g in range(N_BC_GROUPS)]

    li = lax.broadcasted_iota(I32, (q, q), 0)
    si = lax.broadcasted_iota(I32, (q, q), 1)
    dls = li - si
    causal = jnp.where(fwd, dls, -dls) >= 0
    g_parts = [jnp.where(causal, g_c[pl.ds(r0g + g * q, q), :], jnp.zeros((q, q), BF16))
               for g in range(N_BC_GROUPS)]

    src_exp = ra_ref[0]
    w_t = rw_ref[0].astype(BF16)
    col = cc_ref[0]

    lane_q = lax.broadcasted_iota(I32, (q, QUAD_W), 1) // HEADDIM
    y_parts, s_parts = [], []
    for qd in range(N_HEADS // HEAD_QUAD):
        top, bot = [], []
        for j in range(HEAD_QUAD):
            h = qd * HEAD_QUAD + j
            g = h // HEADS_PER_BC
            seg = jnp.minimum(col[:, h:h + 1] - src_exp[h:h + 1, :], EXP2_CLAMP)
            top.append(jnp.exp2(seg).astype(BF16) * g_parts[g])
            bot.append(bt_parts[g] * w_t[h:h + 1, :])
        lhs = jnp.concatenate([jnp.concatenate(top, axis=1), jnp.concatenate(bot, axis=1)], axis=0)
        xq = xs_b[:, qd * QUAD_W:(qd + 1) * QUAD_W]
        rhs = jnp.concatenate([jnp.where(lane_q == j, xq, jnp.zeros_like(xq)) for j in range(HEAD_QUAD)], axis=0)
        res = _dot(lhs, rhs)
        y_parts.append(res[:q])
        s_parts.append(res[q:])
    y_diag = jnp.concatenate(y_parts, axis=1)
    st_new = jnp.concatenate(s_parts, axis=1)

    st = st_ref[...]
    st_b = st.astype(BF16)
    half = D_SSD // N_BC_GROUPS
    y_off = jnp.concatenate([_dot(c_parts[g], st_b[:, g * half:(g + 1) * half]) for g in range(N_BC_GROUPS)],
                            axis=1)
    col_hi, col_lo = _split_bf16(col)
    e_cum = _dot(col_hi, e_ref[...]) + _dot(col_lo, e_ref[...])
    y_dir = y_diag + y_off * e_cum
    e_tot = jnp.where(fwd, e_cum[q - 1:q, :], e_cum[0:1, :])
    st_next = st * e_tot + st_new
    st_ref[...] = st_next

    if has_final:
        @pl.when(c == nc - 1)
        def _():
            fin_ref[0, 0] = st_next.T

    @pl.when(ph == 0)
    def _():
        yb_ref[pl.ds(r0, q), :] = y_dir.astype(BF16)

    @pl.when(ph == 1)
    def _():
        skip = xs_b.astype(F32) * (dsk_ref[0:1, :] + dsk_ref[1:2, :])
        y = y_dir + yb_ref[pl.ds(r0, q), :].astype(F32) + skip
        y = y * _silu(z_ref[...].astype(F32))
        y_ref[...] = (_rms(y) * nw_ref[...]).astype(BF16)


def _ssd_call(xbc, ra, rw, cc, z, init, layer, cw, cb, dsk, nw, e_mat, *, n_seq, nc, base, has_final):
    t = xbc.shape[0]
    q = CHUNK
    has_init = init is not None
    last8 = t // 8 - 1

    def chunk_of(s, ph, c):
        return base + s * nc + jnp.where(ph == 1, c, nc - 1 - c)

    def conv_chunk(s, ph, c):
        return base + s * nc + (1 - ph) * (nc - 1 - c)

    main = lambda s, ph, c: (conv_chunk(s, ph, c), 0)
    prev = lambda s, ph, c: (jnp.maximum(conv_chunk(s, ph, c) * (q // 8) - 1, 0), 0)
    nxt = lambda s, ph, c: (jnp.minimum((conv_chunk(s, ph, c) + 1) * (q // 8), last8), 0)
    table = lambda s, ph, c: (1 - ph, chunk_of(s, ph, c), 0)
    const = lambda s, ph, c: (0, 0)

    in_specs = [
        pl.BlockSpec((q, CONV_DIM), main),
        pl.BlockSpec((8, CONV_DIM), prev),
        pl.BlockSpec((8, CONV_DIM), nxt),
        pl.BlockSpec((1, N_HEADS, q), table),
        pl.BlockSpec((1, N_HEADS, q), table),
        pl.BlockSpec((1, q, LANES), table),
        pl.BlockSpec((q, D_SSD), lambda s, ph, c: (base + s * nc + ph * c, 0)),
    ]
    args = [xbc, xbc, xbc, ra, rw, cc, z]
    if has_init:
        in_specs.append(pl.BlockSpec((1, 1, 1, D_SSD, D_STATE), lambda s, ph, c: (s, layer, 1 - ph, 0, 0)))
        args.append(init)
    for a in (cw, cb, dsk, nw, e_mat):
        in_specs.append(pl.BlockSpec(a.shape, const))
        args.append(a)

    out_specs = [pl.BlockSpec((q, D_SSD), lambda s, ph, c: (s * nc + ph * c, 0))]
    out_shape = [jax.ShapeDtypeStruct((n_seq * nc * q, D_SSD), BF16)]
    if has_final:
        out_specs.append(pl.BlockSpec((1, 1, D_SSD, D_STATE), lambda s, ph, c: (s, 1 - ph, 0, 0)))
        out_shape.append(jax.ShapeDtypeStruct((n_seq, 2, D_SSD, D_STATE), F32))

    out = pl.pallas_call(
        functools.partial(_ssd_kernel, nc=nc, has_init=has_init, has_final=has_final),
        grid=(n_seq, 2, nc),
        in_specs=in_specs,
        out_specs=out_specs,
        out_shape=out_shape,
        scratch_shapes=[
            pltpu.VMEM((D_STATE, D_SSD), F32),
            pltpu.VMEM((nc * q, D_SSD), BF16),
            pltpu.VMEM((nc * q, D_SSD), BF16),
            pltpu.VMEM((nc * q, D_BC), BF16),
            pltpu.VMEM((nc * N_BC_GROUPS * q, q), BF16),
            pltpu.VMEM((nc * N_BC_GROUPS * q, q), BF16),
        ],
        compiler_params=pltpu.CompilerParams(dimension_semantics=("arbitrary", "arbitrary", "arbitrary")),
        name="ssd_mixer_nc%d" % nc,
    )(*args)
    return out if has_final else (out[0], None)


def _window_count(idx, w, n):
    hi = jnp.minimum(idx + (w // 2 - 1), n - 1)
    lo = jnp.maximum(idx - w // 2, 0)
    return hi - lo + 1


def _pool_tail(s, cnt, vt, g, pw_ref, ps_ref):
    cnt2 = jnp.concatenate([cnt, cnt], axis=1).astype(F32)
    upre = s / cnt2 - vt.astype(F32)
    gs = slice(g * POOL_GROUP_W, (g + 1) * POOL_GROUP_W)
    return (_dot(upre.astype(BF16), pw_ref[g]) * ps_ref[:, gs]).astype(BF16)


def _pool_seq_kernel(v_ref, band_ref, pw_ref, ps_ref, u_ref):
    n = v_ref.shape[0]
    pos = lax.broadcasted_iota(I32, (n, LANES), 0)
    for g, w in enumerate(POOL_WINDOWS):
        gs = slice(g * POOL_GROUP_W, (g + 1) * POOL_GROUP_W)
        vt = v_ref[:, gs]
        s = _dot(band_ref[g], vt)
        u_ref[:, gs] = _pool_tail(s, _window_count(pos, w, n), vt, g, pw_ref, ps_ref)


def _pool_grid_kernel(v_ref, band_ref, pw_ref, ps_ref, u_ref, p_ref, *, rows):
    tile = 256
    n = rows * GRID_W
    pad = (max(POOL_WINDOWS) // 2) * GRID_W
    p_ref[0:pad, :] = jnp.zeros((pad, POOL_GROUP_W), F32)
    p_ref[pad + n:pad + n + pad, :] = jnp.zeros((pad, POOL_GROUP_W), F32)
    for g, w in enumerate(POOL_WINDOWS):
        gs = slice(g * POOL_GROUP_W, (g + 1) * POOL_GROUP_W)

        def colsum(t, carry):
            r0 = pl.multiple_of(t * tile, tile)
            p_ref[pl.ds(pad + r0, tile), :] = _dot(band_ref[g], v_ref[pl.ds(r0, tile), gs])
            return carry

        lax.fori_loop(0, n // tile, colsum, 0)

        def rowsum(t, carry):
            r0 = pl.multiple_of(t * tile, tile)
            s = p_ref[pl.ds(pad + r0 - (w // 2) * GRID_W, tile), :]
            for d in range(-(w // 2) + 1, w // 2):
                s = s + p_ref[pl.ds(pad + r0 + d * GRID_W, tile), :]
            tok = r0 + lax.broadcasted_iota(I32, (tile, LANES), 0)
            grid_row = jnp.right_shift(tok, GRID_W.bit_length() - 1)
            grid_col = jnp.bitwise_and(tok, GRID_W - 1)
            cnt = _window_count(grid_row, w, rows) * _window_count(grid_col, w, GRID_W)
            vt = v_ref[pl.ds(r0, tile), gs]
            u_ref[pl.ds(r0, tile), gs] = _pool_tail(s, cnt, vt, g, pw_ref, ps_ref)
            return carry

        lax.fori_loop(0, n // tile, rowsum, 0)


def _band_matrix(n, w):
    idx = np.arange(n)
    lo = np.clip(idx - w // 2, 0, n - 1)
    hi = np.clip(idx + w // 2 - 1, 0, n - 1)
    j = idx[None, :]
    return ((j >= lo[:, None]) & (j <= hi[:, None])).astype(np.float32)


def _pool_call(v, pw, ps, *, n_seq, seq_len, base_block, grid_rows):
    if grid_rows is None:
        bands = np.stack([_band_matrix(seq_len, w) for w in POOL_WINDOWS])
        kern = _pool_seq_kernel
        scratch = []
    else:
        per = 256 // GRID_W
        bands = np.stack([np.kron(np.eye(per, dtype=np.float32), _band_matrix(GRID_W, w)) for w in POOL_WINDOWS])
        kern = functools.partial(_pool_grid_kernel, rows=grid_rows)
        pad = (max(POOL_WINDOWS) // 2) * GRID_W
        scratch = [pltpu.VMEM((seq_len + 2 * pad, POOL_GROUP_W), F32)]
    bands = jnp.asarray(bands, BF16)
    const3 = lambda s: (0, 0, 0)
    return pl.pallas_call(
        kern,
        grid=(n_seq,),
        in_specs=[
            pl.BlockSpec((seq_len, D_MODEL), lambda s: (base_block + s, 0)),
            pl.BlockSpec(bands.shape, const3),
            pl.BlockSpec(pw.shape, const3),
            pl.BlockSpec(ps.shape, lambda s: (0, 0)),
        ],
        out_specs=pl.BlockSpec((seq_len, D_MODEL), lambda s: (s, 0)),
        out_shape=jax.ShapeDtypeStruct((n_seq * seq_len, D_MODEL), BF16),
        scratch_shapes=scratch,
        name="pool_mixer_len%d" % seq_len,
    )(v, bands, pw, ps)


def _route_kernel(yp_ref, ys_ref, up_ref, us_ref, xp_ref, xs_ref, mod_ref, wy_ref, wu_ref, nw_ref, wr_ref, rb_ref,
                  x1_ref, h2_ref, cls_ref, *, prompt_tiles):
    tm = x1_ref.shape[0]
    is_prompt = pl.program_id(0) < prompt_tiles
    y = jnp.where(is_prompt, yp_ref[...], ys_ref[...])
    u = jnp.where(is_prompt, up_ref[...], us_ref[...])
    x = jnp.where(is_prompt, xp_ref[...], xs_ref[...])
    mix = _dot(y, wy_ref[0]) + _dot(u, wu_ref[0])
    x1 = x + mod_ref[0, 2:3, :] * mix
    x1_ref[...] = x1
    h2 = (_rms(x1) * nw_ref[...]) * (1.0 + mod_ref[0, 4:5, :]) + mod_ref[0, 3:4, :]
    _store_slabs(h2_ref, h2)

    logits = _dot_split(h2, wr_ref[...]).T[0:N_EXPERTS, :]
    e = jnp.exp(logits - jnp.max(logits, axis=0, keepdims=True))
    probs = e / jnp.sum(e, axis=0, keepdims=True)
    sel = probs + rb_ref[...]
    s_rows = [sel[i:i + 1, :] for i in range(N_EXPERTS)]

    scores = []
    for g in range(N_EXPERT_GROUPS):
        a, b, c, d = s_rows[4 * g:4 * g + 4]
        hi1, lo1 = jnp.maximum(a, b), jnp.minimum(a, b)
        hi2, lo2 = jnp.maximum(c, d), jnp.minimum(c, d)
        scores.append(jnp.maximum(hi1, hi2) + jnp.maximum(jnp.minimum(hi1, hi2), jnp.maximum(lo1, lo2)))
    best_g = jnp.zeros((1, tm), I32)
    best_s = scores[0]
    for g in range(1, N_EXPERT_GROUPS):
        better = scores[g] > best_s
        best_g = jnp.where(better, g, best_g)
        best_s = jnp.where(better, scores[g], best_s)

    def in_group(rows, j):
        out = rows[j]
        for g in range(1, N_EXPERT_GROUPS):
            out = jnp.where(best_g == g, rows[4 * g + j], out)
        return out

    sg = [in_group(s_rows, j) for j in range(EXPERTS_PER_GROUP)]
    i1 = jnp.zeros((1, tm), I32)
    v1 = sg[0]
    for j in range(1, EXPERTS_PER_GROUP):
        better = sg[j] > v1
        i1 = jnp.where(better, j, i1)
        v1 = jnp.where(better, sg[j], v1)
    i2 = jnp.zeros((1, tm), I32)
    v2 = jnp.where(i1 == 0, NEG_INF, sg[0])
    for j in range(1, EXPERTS_PER_GROUP):
        cand = jnp.where(i1 == j, NEG_INF, sg[j])
        better = cand > v2
        i2 = jnp.where(better, j, i2)
        v2 = jnp.where(better, cand, v2)

    lo = jnp.minimum(i1, i2)
    hi = jnp.maximum(i1, i2)
    pair = jnp.where(lo == 0, hi - 1, jnp.where(lo == 1, hi + 1, 5))
    cls_ref[0] = best_g * len(EXPERT_PAIRS) + pair


def _route_call(yp, ys, up, us, xp, xs, mod, w_out, layer, nw, wr, rb, mod_row):
    t_p, t_s = xp.shape[0], xs.shape[0]
    t = t_p + t_s
    tm = TM_TOK
    row = lambda i: (i, 0)
    const = lambda i: (0, 0)
    return pl.pallas_call(
        functools.partial(_route_kernel, prompt_tiles=t_p // tm),
        grid=(t // tm,),
        in_specs=_group_specs(tm, D_SSD, t_p, t_s) + _group_specs(tm, D_MODEL, t_p, t_s)
        + _group_specs(tm, D_MODEL, t_p, t_s) + [
            pl.BlockSpec((1, 8, D_MODEL), lambda i: (mod_row(i * tm), 0, 0)),
            pl.BlockSpec((1, D_SSD, D_MODEL), lambda i: (layer, 0, 0)),
            pl.BlockSpec((1, D_MODEL, D_MODEL), lambda i: (layer, D_SSD // D_MODEL, 0)),
            pl.BlockSpec((1, D_MODEL), const),
            pl.BlockSpec(wr.shape, const),
            pl.BlockSpec(rb.shape, const),
        ],
        out_specs=[
            pl.BlockSpec((tm, D_MODEL), row),
            pl.BlockSpec((tm * SLAB, LANES), row),
            pl.BlockSpec((1, 1, tm), lambda i: (i, 0, 0)),
        ],
        out_shape=[
            jax.ShapeDtypeStruct((t, D_MODEL), F32),
            jax.ShapeDtypeStruct((t * SLAB, LANES), F32),
            jax.ShapeDtypeStruct((t // tm, 1, tm), I32),
        ],
        name="out_proj_router",
    )(yp, ys, up, us, xp, xs, mod, w_out, w_out, nw, wr, rb)


def _sort_kernel(cls_ref, pos_ref, tile_ref):
    cls = cls_ref[...]
    nr = cls.shape[0]
    ci = lax.broadcasted_iota(I32, (LANES, LANES), 0)
    cj = lax.broadcasted_iota(I32, (LANES, LANES), 1)
    upper = jnp.where(ci <= cj, 1.0, 0.0).astype(BF16)
    ri = lax.broadcasted_iota(I32, (nr, nr), 0)
    rj = lax.broadcasted_iota(I32, (nr, nr), 1)
    lower = jnp.where(rj < ri, 1.0, 0.0).astype(BF16)
    tile_idx = lax.broadcasted_iota(I32, (8, LANES), 1).astype(F32)
    pos = jnp.zeros(cls.shape, F32)
    tile_cls = jnp.zeros((8, LANES), F32)
    tiles_done = jnp.zeros((1, 1), F32)
    for k in range(N_CLASSES):
        member = cls == k
        mf = jnp.where(member, 1.0, 0.0)
        within = _dot(mf.astype(BF16), upper)
        row_tot = jnp.broadcast_to(within[:, LANES - 1:LANES], cls.shape)
        before = _dot(lower, row_tot.astype(BF16))
        rank = within - mf + before
        count = jnp.sum(jnp.sum(mf, axis=1, keepdims=True), axis=0, keepdims=True)
        n_tiles = jnp.floor((count + (TM_EXP - 1)) * (1.0 / TM_EXP))
        pos = pos + jnp.where(member, tiles_done * TM_EXP + rank, 0.0)
        tiles_done = tiles_done + n_tiles
        tile_cls = tile_cls + jnp.where(tile_idx >= tiles_done, 1.0, 0.0)
    pos_ref[...] = pos.astype(I32)
    r = lax.broadcasted_iota(I32, (8, LANES), 0)
    tile_ref[...] = jnp.where(r == 1, tiles_done, tile_cls).astype(I32)


def _sort_call(cls):
    t = cls.size
    rows = -(-t // (LANES * LANES)) * LANES
    cls2 = jnp.pad(cls.reshape(t // LANES, LANES), ((0, rows - t // LANES), (0, 0)), constant_values=N_CLASSES)
    pos, tiles = pl.pallas_call(
        _sort_kernel,
        out_shape=[jax.ShapeDtypeStruct(cls2.shape, I32), jax.ShapeDtypeStruct((8, LANES), I32)],
        name="route_sort",
    )(cls2)
    return pos.reshape(-1)[:t], tiles


def _slab(ref, tok):
    return ref.at[pl.ds(pl.multiple_of(tok * SLAB, SLAB), SLAB), :]


def _scatter_kernel(pos_ref, src_ref, dst_in, dst_ref, sem):
    del dst_in
    tm = src_ref.shape[0] // SLAB
    base = pl.program_id(0) * tm

    def row_copy(r):
        return pltpu.make_async_copy(_slab(src_ref, r), _slab(dst_ref, pos_ref[base + r]), sem)

    def body(r, carry):
        row_copy(r).start()
        return carry

    lax.fori_loop(0, tm, body, 0, unroll=DMA_UNROLL)

    def drain(r, carry):
        row_copy(r).wait()
        return carry

    lax.fori_loop(0, tm, drain, 0, unroll=DMA_UNROLL)


def _scatter_call(pos, src, n_tokens):
    tm = TM_TOK
    t = src.shape[0] // SLAB
    dst = jnp.zeros((n_tokens * SLAB, LANES), src.dtype)
    return pl.pallas_call(
        _scatter_kernel,
        grid_spec=pltpu.PrefetchScalarGridSpec(
            num_scalar_prefetch=1,
            grid=(t // tm,),
            in_specs=[pl.BlockSpec((tm * SLAB, LANES), lambda i, pos: (i, 0)),
                      pl.BlockSpec(memory_space=pl.ANY)],
            out_specs=pl.BlockSpec(memory_space=pl.ANY),
            scratch_shapes=[pltpu.SemaphoreType.DMA(())],
        ),
        out_shape=jax.ShapeDtypeStruct(dst.shape, dst.dtype),
        input_output_aliases={2: 0},
        name="route_scatter",
    )(pos, src, dst)


def _unpermute_kernel(pos_ref, x1_ref, mod_ref, nw_ref, y_hbm, op_ref, os_ref, buf, sem, *, final_norm,
                      prompt_tiles):
    tm = x1_ref.shape[0]
    base = pl.program_id(0) * tm

    def row_copy(r):
        return pltpu.make_async_copy(_slab(y_hbm, pos_ref[base + r]), _slab(buf, r), sem)

    def body(r, carry):
        row_copy(r).start()
        return carry

    lax.fori_loop(0, tm, body, 0, unroll=DMA_UNROLL)

    def drain(r, carry):
        row_copy(r).wait()
        return carry

    lax.fori_loop(0, tm, drain, 0, unroll=DMA_UNROLL)
    x2 = x1_ref[...] + mod_ref[0, 5:6, :] * _load_slabs(buf, tm)
    if final_norm:
        x2 = _rms(x2) * nw_ref[...]

    @pl.when(pl.program_id(0) < prompt_tiles)
    def _():
        op_ref[...] = x2

    @pl.when(pl.program_id(0) >= prompt_tiles)
    def _():
        os_ref[...] = x2


def _unpermute_call(pos, x1, mod, nw, y_sorted, mod_row, t_p, final_norm):
    t = x1.shape[0]
    tm = TM_TOK
    return pl.pallas_call(
        functools.partial(_unpermute_kernel, final_norm=final_norm, prompt_tiles=t_p // tm),
        grid_spec=pltpu.PrefetchScalarGridSpec(
            num_scalar_prefetch=1,
            grid=(t // tm,),
            in_specs=[pl.BlockSpec((tm, D_MODEL), lambda i, pos: (i, 0)),
                      pl.BlockSpec((1, 8, D_MODEL), lambda i, pos: (mod_row(i * tm), 0, 0)),
                      pl.BlockSpec((1, D_MODEL), lambda i, pos: (0, 0)),
                      pl.BlockSpec(memory_space=pl.ANY)],
            out_specs=_group_specs(tm, D_MODEL, t_p, t - t_p),
            scratch_shapes=[pltpu.VMEM((tm * SLAB, LANES), F32), pltpu.SemaphoreType.DMA(())],
        ),
        out_shape=[jax.ShapeDtypeStruct((t_p, D_MODEL), F32), jax.ShapeDtypeStruct((t - t_p, D_MODEL), F32)],
        name="moe_unpermute",
    )(pos, x1, mod, nw, y_sorted)


def _moe_kernel(ea_ref, eb_ref, nact_ref, x_ref, wr_ref, wga, wua, wda, wgb, wub, wdb, y_ref):
    j = pl.program_id(0)
    tm = x_ref.shape[0] // SLAB

    @pl.when(j < nact_ref[0])
    def _():
        x = _load_slabs(x_ref, tm).astype(BF16)
        logits = _dot(x, wr_ref[...])
        lane = lax.broadcasted_iota(I32, logits.shape, 1)
        l_a = jnp.sum(jnp.where(lane == ea_ref[j], logits, 0.0), axis=1, keepdims=True)
        l_b = jnp.sum(jnp.where(lane == eb_ref[j], logits, 0.0), axis=1, keepdims=True)
        gate_a = 1.0 / (1.0 + jnp.exp(l_b - l_a))
        gate_b = 1.0 / (1.0 + jnp.exp(l_a - l_b))

        def ffn(wg, wu, wd, gate):
            g = _dot(x, wg[0, 0].astype(BF16))
            u = _dot(x, wu[0, 0].astype(BF16))
            a = _silu(g) * u * gate
            return _dot(a.astype(BF16), wd[0, 0].astype(BF16))

        _store_slabs(y_ref, ffn(wga, wua, wda, gate_a) + ffn(wgb, wub, wdb, gate_b))

    @pl.when(j >= nact_ref[0])
    def _():
        y_ref[...] = jnp.zeros_like(y_ref)


def _moe_call(ea, eb, nact, xs, wr, w_gate, w_up, w_down, layer):
    n_rows = xs.shape[0] // SLAB
    tm = TM_EXP
    wa = lambda j, ea, eb, na: (layer, ea[j], 0, 0)
    wb = lambda j, ea, eb, na: (layer, eb[j], 0, 0)
    gu = (1, 1, D_MODEL, D_FF)
    dn = (1, 1, D_FF, D_MODEL)
    tile = pl.BlockSpec((tm * SLAB, LANES), lambda j, ea, eb, na: (j, 0))
    return pl.pallas_call(
        _moe_kernel,
        grid_spec=pltpu.PrefetchScalarGridSpec(
            num_scalar_prefetch=3,
            grid=(n_rows // tm,),
            in_specs=[tile, pl.BlockSpec(wr.shape, lambda j, ea, eb, na: (0, 0)),
                      pl.BlockSpec(gu, wa), pl.BlockSpec(gu, wa), pl.BlockSpec(dn, wa),
                      pl.BlockSpec(gu, wb), pl.BlockSpec(gu, wb), pl.BlockSpec(dn, wb)],
            out_specs=tile,
        ),
        out_shape=jax.ShapeDtypeStruct((n_rows * SLAB, LANES), F32),
        name="moe_ffn",
    )(ea, eb, nact, xs, wr, w_gate, w_up, w_down, w_gate, w_up, w_down)


def kernel(x_prompt, x_sample, state_ssm, c, c_ctx, w_ada, b_ada, norm1_w, w_in, conv_w, conv_b, dt_bias, a_log, d_skip, ssd_norm_w, pool_w, pool_scale, w_out, norm2_w, w_router, router_bias, w_gate, w_up, w_down, final_norm_w):
    n_p, len_p, d = x_prompt.shape
    n_s, len_s, _ = x_sample.shape
    depth = w_ada.shape[0]
    t_p, t_s = n_p * len_p, n_s * len_s
    t = t_p + t_s
    assert d == D_MODEL and len_p % TM_TOK == 0 and len_s % TM_PROJ == 0 and t_p % TM_PROJ == 0
    assert len_s % GRID_W == 0 and t_p % len_s == 0

    def mod_row(tok):
        return jnp.where(tok < t_p, 0, 1 + (tok - t_p) // len_s)

    xp, xs = x_prompt.reshape(t_p, d), x_sample.reshape(t_s, d)

    cvec = jnp.zeros((8, d), F32).at[0].set(c_ctx).at[1:1 + n_s].set(c)
    ada = _ada_call(cvec, w_ada, b_ada)
    mods = jnp.pad(ada[:, :1 + n_s].reshape(depth, 1 + n_s, 6, d), ((0, 0), (0, 0), (0, 2), (0, 0)))

    e_np = np.zeros((LANES, D_SSD), np.float32)
    e_np[N_HEADS:2 * N_HEADS] = np.kron(np.eye(N_HEADS, dtype=np.float32), np.ones((1, HEADDIM), np.float32))
    e_mat = jnp.asarray(e_np, BF16)
    wr = jnp.pad(w_router, ((0, 0), (0, LANES - N_EXPERTS)))
    rb = jnp.broadcast_to(router_bias[:, None], (N_EXPERTS, TM_TOK))
    pairs = np.asarray(EXPERT_PAIRS, np.int32)
    n_sorted = t + N_CLASSES * TM_EXP
    n_tiles = n_sorted // TM_EXP
    assert n_tiles <= LANES

    init_all = state_ssm.reshape(n_s, depth, 2, D_SSD, D_STATE)
    w_out_b = w_out.astype(BF16)

    new_states = []
    for l in range(depth):
        w_l = w_in[l]
        wz = w_l[:, :D_SSD].astype(BF16)
        wx = w_l[:, D_SSD:D_SSD + CONV_DIM].astype(BF16)
        o = D_SSD + CONV_DIM
        wd = jnp.pad(w_l[:, o:o + 2 * N_HEADS], ((0, 0), (0, LANES - 2 * N_HEADS))).astype(BF16)
        wv = w_l[:, o + 2 * N_HEADS:].astype(BF16)
        z, xbc, v, dt = _inproj_call(xp, xs, mods[l], norm1_w[l][None], wz, wx, wv, wd, mod_row)

        dtb = jnp.broadcast_to(dt_bias[l].reshape(2 * N_HEADS, 1), (2 * N_HEADS, CHUNK))
        alog = jnp.broadcast_to(a_log[l].reshape(2 * N_HEADS, 1), (2 * N_HEADS, CHUNK))
        dsk = jnp.repeat(d_skip[l], HEADDIM, axis=1)
        ra, rw, cc = _decay_call(dt, dtb, alog)
        ssd_args = (conv_w[l], conv_b[l][None], dsk, ssd_norm_w[l][None], e_mat)
        y_p, fin = _ssd_call(xbc, ra, rw, cc, z, None, l, *ssd_args, n_seq=n_p, nc=len_p // CHUNK, base=0,
                             has_final=True)
        y_s, _ = _ssd_call(xbc, ra, rw, cc, z, init_all, l, *ssd_args, n_seq=n_s, nc=len_s // CHUNK,
                           base=t_p // CHUNK, has_final=False)
        new_states.append(fin)

        pw = pool_w[l].astype(BF16)
        ps = pool_scale[l][None]
        u_p = _pool_call(v, pw, ps, n_seq=n_p, seq_len=len_p, base_block=0, grid_rows=None)
        u_s = _pool_call(v, pw, ps, n_seq=n_s, seq_len=len_s, base_block=t_p // len_s, grid_rows=len_s // GRID_W)

        x1, h2, cls = _route_call(y_p, y_s, u_p, u_s, xp, xs, mods[l], w_out_b, l, norm2_w[l][None], wr, rb, mod_row)

        pos, tiles = _sort_call(cls)
        nact = tiles[1, 0:1]
        tcls = tiles[0, :n_tiles]
        tcls = jnp.minimum(tcls, tcls[jnp.maximum(nact[0] - 1, 0)])
        tcls = jnp.minimum(tcls, N_CLASSES - 1)
        grp = tcls // len(EXPERT_PAIRS)
        pr = tcls % len(EXPERT_PAIRS)
        ea = grp * EXPERTS_PER_GROUP + jnp.asarray(pairs[:, 0])[pr]
        eb = grp * EXPERTS_PER_GROUP + jnp.asarray(pairs[:, 1])[pr]

        h2s = _scatter_call(pos, h2, n_sorted)
        y_sorted = _moe_call(ea, eb, nact, h2s, wr.astype(BF16), w_gate, w_up, w_down, l)
        last = l == depth - 1
        xp, xs = _unpermute_call(pos, x1, mods[l], final_norm_w[None], y_sorted, mod_row, t_p, last)

    y_prompt = xp.reshape(n_p, len_p, d)
    y_sample = xs.reshape(n_s, len_s, d)
    ns = jnp.stack(new_states, axis=1)
    new_state_ssm = ns.reshape(n_p, depth, 2, N_HEADS, HEADDIM, D_STATE)
    return (y_prompt, y_sample, new_state_ssm)
```

```python
import functools

import numpy as np
import jax
import jax.numpy as jnp
from jax import lax
from jax.experimental import pallas as pl
from jax.experimental.pallas import tpu as pltpu

F32 = jnp.float32
BF16 = jnp.bfloat16
I32 = jnp.int32

D_MODEL = 1024
HEADDIM = 64
N_HEADS = 16
N_BC_GROUPS = 2
HEADS_PER_BC = N_HEADS // N_BC_GROUPS
D_STATE = 128
CHUNK = 128
D_SSD = N_HEADS * HEADDIM
D_BC = N_BC_GROUPS * D_STATE
CONV_DIM = D_SSD + 2 * D_BC
POOL_WINDOWS = (2, 4, 8, 16)
POOL_GROUP_W = 256
GRID_W = 64
N_EXPERTS = 16
EXPERTS_PER_GROUP = 4
N_EXPERT_GROUPS = N_EXPERTS // EXPERTS_PER_GROUP
D_FF = 512
EPS = 1e-6

LANES = 128
HEAD_QUAD = 4
QUAD_W = HEAD_QUAD * HEADDIM
EXPERT_PAIRS = ((0, 1), (0, 2), (0, 3), (1, 2), (1, 3), (2, 3))
N_CLASSES = N_EXPERT_GROUPS * len(EXPERT_PAIRS)

TM_PROJ = 512
TM_TOK = 256
TM_EXP = 256
SLAB = D_MODEL // LANES
DMA_UNROLL = 8
DMA_THREADS = 2
NEG_INF = float("-inf")


def _dot(a, b):
    return jnp.dot(a, b, preferred_element_type=F32)


def _split_bf16(x):
    hi = x.astype(BF16)
    lo = (x - hi.astype(F32)).astype(BF16)
    return hi, lo


def _dot_split(a, w):
    ah, al = _split_bf16(a)
    wh, wl = _split_bf16(w)
    return _dot(ah, wh) + (_dot(al, wh) + _dot(ah, wl))


def _silu(x):
    return x / (1.0 + jnp.exp(-x))


def _rms(x):
    return x * lax.rsqrt(jnp.mean(x * x, axis=-1, keepdims=True) + EPS)


def _store_slabs(ref, val):
    n = val.shape[0]
    for k in range(SLAB):
        ref[pl.ds(k, n, stride=SLAB), :] = val[:, k * LANES:(k + 1) * LANES]


def _load_slabs(ref, n):
    return jnp.concatenate([ref[pl.ds(k, n, stride=SLAB), :] for k in range(SLAB)], axis=1)


def _ada_kernel(c_ref, w_ref, b_ref, o_ref):
    o_ref[0] = _dot_split(_silu(c_ref[...]), w_ref[0]) + b_ref[0]


def _ada_call(cvec, w_ada, b_ada):
    depth, d, n = w_ada.shape
    tn = 1024
    return pl.pallas_call(
        _ada_kernel,
        grid=(depth, n // tn),
        in_specs=[
            pl.BlockSpec((8, d), lambda l, j: (0, 0)),
            pl.BlockSpec((1, d, tn), lambda l, j: (l, 0, j)),
            pl.BlockSpec((1, 1, tn), lambda l, j: (l, 0, j)),
        ],
        out_specs=pl.BlockSpec((1, 8, tn), lambda l, j: (l, 0, j)),
        out_shape=jax.ShapeDtypeStruct((depth, 8, n), F32),
        name="ada_ln",
    )(cvec, w_ada, b_ada.reshape(depth, 1, n))


def _group_specs(tm, width, p_rows, s_rows):
    pt, st = p_rows // tm, s_rows // tm
    return [pl.BlockSpec((tm, width), lambda i, *_: (jnp.minimum(i, pt - 1), 0)),
            pl.BlockSpec((tm, width), lambda i, *_: (jnp.clip(i - pt, 0, st - 1), 0))]


def _inproj_kernel(xp_ref, xs_ref, mod_ref, nw_ref, wz_ref, wx_ref, wv_ref, wd_ref,
                   z_ref, xbc_ref, v_ref, dt_ref, *, prompt_tiles):
    x = jnp.where(pl.program_id(0) < prompt_tiles, xp_ref[...], xs_ref[...])
    h = _rms(x) * nw_ref[...]
    h = h * (1.0 + mod_ref[0, 1:2, :]) + mod_ref[0, 0:1, :]
    hb = h.astype(BF16)
    z_ref[...] = _dot(hb, wz_ref[...]).astype(BF16)
    xbc_ref[...] = _dot(hb, wx_ref[...])
    v_ref[...] = _dot(hb, wv_ref[...]).astype(BF16)
    dt_ref[...] = _dot(hb, wd_ref[...])


def _inproj_call(xp, xs, mod, nw, wz, wx, wv, wd, mod_row):
    t = xp.shape[0] + xs.shape[0]
    tm = TM_PROJ
    const = lambda i: (0, 0)
    row = lambda i: (i, 0)
    return pl.pallas_call(
        functools.partial(_inproj_kernel, prompt_tiles=xp.shape[0] // tm),
        grid=(t // tm,),
        in_specs=_group_specs(tm, D_MODEL, xp.shape[0], xs.shape[0]) + [
            pl.BlockSpec((1, 8, D_MODEL), lambda i: (mod_row(i * tm), 0, 0)),
            pl.BlockSpec((1, D_MODEL), const),
            pl.BlockSpec(wz.shape, const),
            pl.BlockSpec(wx.shape, const),
            pl.BlockSpec(wv.shape, const),
            pl.BlockSpec(wd.shape, const),
        ],
        out_specs=[
            pl.BlockSpec((tm, D_SSD), row),
            pl.BlockSpec((tm, CONV_DIM), row),
            pl.BlockSpec((tm, D_MODEL), row),
            pl.BlockSpec((tm, LANES), row),
        ],
        out_shape=[
            jax.ShapeDtypeStruct((t, D_SSD), BF16),
            jax.ShapeDtypeStruct((t, CONV_DIM), F32),
            jax.ShapeDtypeStruct((t, D_MODEL), BF16),
            jax.ShapeDtypeStruct((t, LANES), F32),
        ],
        name="in_proj",
    )(xp, xs, mod, nw, wz, wx, wv, wd)


LOG2E = 1.4426950408889634
PREP_CHUNKS = 8


def _decay_kernel(dt_ref, dtb_ref, alog_ref, ra_ref, rw_ref, cc_ref):
    q = CHUNK
    ci = lax.broadcasted_iota(I32, (q, q), 0)
    cj = lax.broadcasted_iota(I32, (q, q), 1)
    upper = jnp.where(ci <= cj, 1.0, 0.0).astype(BF16)
    t_parts = [dt_ref[k * q:(k + 1) * q, :].T for k in range(PREP_CHUNKS)]
    for d in range(2):
        x = jnp.concatenate([t[d * N_HEADS:(d + 1) * N_HEADS, :] for t in t_parts], axis=0)
        tile = lambda r: jnp.concatenate([r[d * N_HEADS:(d + 1) * N_HEADS, :]] * PREP_CHUNKS, axis=0)
        xdt = x + tile(dtb_ref)
        dtv = jnp.maximum(xdt, 0.0) + jnp.log(1.0 + jnp.exp(-jnp.abs(xdt)))
        a_step = dtv * (-jnp.exp(tile(alog_ref)))
        hi = a_step.astype(BF16)
        r1 = a_step - hi.astype(F32)
        mid = r1.astype(BF16)
        lo = (r1 - mid.astype(F32)).astype(BF16)
        pref = _dot(hi, upper) + (_dot(mid, upper) + _dot(lo, upper))
        total = pref[:, q - 1:q]
        cum = pref if d == 0 else total - pref + a_step
        ra_ref[d] = (cum - jnp.log(dtv)) * LOG2E
        rw_ref[d] = dtv * jnp.exp(total - cum)
        cum2 = cum * LOG2E
        e_cum = jnp.exp(cum)
        pad = jnp.zeros((q - 2 * N_HEADS, q), F32)
        for k in range(PREP_CHUNKS):
            rs = slice(k * N_HEADS, (k + 1) * N_HEADS)
            cc_ref[d, k * q:(k + 1) * q, :] = jnp.concatenate([cum2[rs], e_cum[rs], pad], axis=0).T


def _decay_call(dt, dtb, alog):
    t = dt.shape[0]
    q = CHUNK
    tm = PREP_CHUNKS * q
    rows = PREP_CHUNKS * N_HEADS
    n_rows = t // q * N_HEADS
    const = lambda i: (0, 0)
    return pl.pallas_call(
        _decay_kernel,
        grid=(t // tm,),
        in_specs=[pl.BlockSpec((tm, LANES), lambda i: (i, 0)),
                  pl.BlockSpec(dtb.shape, const),
                  pl.BlockSpec(alog.shape, const)],
        out_specs=[pl.BlockSpec((2, rows, q), lambda i: (0, i, 0)),
                   pl.BlockSpec((2, rows, q), lambda i: (0, i, 0)),
                   pl.BlockSpec((2, tm, LANES), lambda i: (0, i, 0))],
        out_shape=[jax.ShapeDtypeStruct((2, n_rows, q), F32),
                   jax.ShapeDtypeStruct((2, n_rows, q), F32),
                   jax.ShapeDtypeStruct((2, t, LANES), F32)],
        name="ssd_decay_tables",
    )(dt, dtb, alog)


EXP2_CLAMP = 100.0


def _ssd_kernel(*refs, nc, has_init, has_final):
    it = iter(refs)
    xm_ref, xp_ref, xn_ref, ra_ref, rw_ref, cc_ref, z_ref = (next(it) for _ in range(7))
    init_ref = next(it) if has_init else None
    cw_ref, cb_ref, dsk_ref, nw_ref, e_ref = (next(it) for _ in range(5))
    y_ref = next(it)
    fin_ref = next(it) if has_final else None
    st_ref, yb_ref, xs_c, c_c, bt_c, g_c = (next(it) for _ in range(6))

    ph = pl.program_id(1)
    c = pl.program_id(2)
    fwd = ph == 1
    cc = jnp.where(fwd, c, nc - 1 - c)
    q = CHUNK
    r0 = pl.multiple_of(cc * q, q)
    r0g = pl.multiple_of(cc * (N_BC_GROUPS * q), N_BC_GROUPS * q)

    @pl.when(c == 0)
    def _():
        if has_init:
            st_ref[...] = init_ref[0, 0, 0].T
        else:
            st_ref[...] = jnp.zeros_like(st_ref)

    @pl.when(ph == 0)
    def _():
        xm = xm_ref[...]
        prev_row = jnp.where(cc > 0, xp_ref[7:8, :], 0.0)
        next0 = jnp.where(cc < nc - 1, xn_ref[0:1, :], 0.0)
        next1 = jnp.where(cc < nc - 1, xn_ref[1:2, :], 0.0)
        rows = lax.broadcasted_iota(I32, (q, 1), 0)
        x_m1 = jnp.where(rows == 0, prev_row, pltpu.roll(xm, 1, 0))
        x_p1 = jnp.where(rows == q - 1, next0, pltpu.roll(xm, q - 1, 0))
        x_p2 = jnp.where(rows == q - 2, next0, jnp.where(rows == q - 1, next1, pltpu.roll(xm, q - 2, 0)))
        conv = (cw_ref[0:1, :] * x_m1 + cw_ref[1:2, :] * xm + cw_ref[2:3, :] * x_p1
                + cw_ref[3:4, :] * x_p2 + cb_ref[...])
        act = _silu(conv)
        xs_c[pl.ds(r0, q), :] = act[:, :D_SSD].astype(BF16)
        c_c[pl.ds(r0, q), :] = act[:, D_SSD + D_BC:].astype(BF16)
        for g in range(N_BC_GROUPS):
            b_g = act[:, D_SSD + g * D_STATE:D_SSD + (g + 1) * D_STATE]
            c_g = act[:, D_SSD + D_BC + g * D_STATE:D_SSD + D_BC + (g + 1) * D_STATE].astype(BF16)
            bt_c[pl.ds(r0g + g * q, q), :] = b_g.T.astype(BF16)
            g_c[pl.ds(r0g + g * q, q), :] = lax.dot_general(
                c_g, b_g.astype(BF16), (((1,), (1,)), ((), ())), preferred_element_type=F32).astype(BF16)

    xs_b = xs_c[pl.ds(r0, q), :]
    c_all = c_c[pl.ds(r0, q), :]
    c_parts = [c_all[:, g * D_STATE:(g + 1) * D_STATE] for g in range(N_BC_GROUPS)]
    bt_parts = [bt_c[pl.ds(r0g + g * q, q), :] for g in range(N_BC_GROUPS)]

    li = lax.broadcasted_iota(I32, (q, q), 0)
    si = lax.broadcasted_iota(I32, (q, q), 1)
    dls = li - si
    causal = jnp.where(fwd, dls, -dls) >= 0
    g_parts = [jnp.where(causal, g_c[pl.ds(r0g + g * q, q), :], jnp.zeros((q, q), BF16))
               for g in range(N_BC_GROUPS)]

    src_exp = ra_ref[0]
    w_t = rw_ref[0].astype(BF16)
    col = cc_ref[0]

    lane_q = lax.broadcasted_iota(I32, (q, QUAD_W), 1) // HEADDIM
    y_parts, s_parts = [], []
    for qd in range(N_HEADS // HEAD_QUAD):
        top, bot = [], []
        for j in range(HEAD_QUAD):
            h = qd * HEAD_QUAD + j
            g = h // HEADS_PER_BC
            seg = jnp.minimum(col[:, h:h + 1] - src_exp[h:h + 1, :], EXP2_CLAMP)
            top.append(jnp.exp2(seg).astype(BF16) * g_parts[g])
            bot.append(bt_parts[g] * w_t[h:h + 1, :])
        lhs = jnp.concatenate([jnp.concatenate(top, axis=1), jnp.concatenate(bot, axis=1)], axis=0)
        xq = xs_b[:, qd * QUAD_W:(qd + 1) * QUAD_W]
        rhs = jnp.concatenate([jnp.where(lane_q == j, xq, jnp.zeros_like(xq)) for j in range(HEAD_QUAD)], axis=0)
        res = _dot(lhs, rhs)
        y_parts.append(res[:q])
        s_parts.append(res[q:])
    y_diag = jnp.concatenate(y_parts, axis=1)
    st_new = jnp.concatenate(s_parts, axis=1)

    st = st_ref[...]
    st_b = st.astype(BF16)
    half = D_SSD // N_BC_GROUPS
    y_off = jnp.concatenate([_dot(c_parts[g], st_b[:, g * half:(g + 1) * half]) for g in range(N_BC_GROUPS)],
                            axis=1)
    col_hi, col_lo = _split_bf16(col)
    e_cum = _dot(col_hi, e_ref[...]) + _dot(col_lo, e_ref[...])
    y_dir = y_diag + y_off * e_cum
    e_tot = jnp.where(fwd, e_cum[q - 1:q, :], e_cum[0:1, :])
    st_next = st * e_tot + st_new
    st_ref[...] = st_next

    if has_final:
        @pl.when(c == nc - 1)
        def _():
            fin_ref[0, 0] = st_next.T

    @pl.when(ph == 0)
    def _():
        yb_ref[pl.ds(r0, q), :] = y_dir.astype(BF16)

    @pl.when(ph == 1)
    def _():
        skip = xs_b.astype(F32) * (dsk_ref[0:1, :] + dsk_ref[1:2, :])
        y = y_dir + yb_ref[pl.ds(r0, q), :].astype(F32) + skip
        y = y * _silu(z_ref[...].astype(F32))
        y_ref[...] = (_rms(y) * nw_ref[...]).astype(BF16)


def _ssd_call(xbc, ra, rw, cc, z, init, layer, cw, cb, dsk, nw, e_mat, *, n_seq, nc, base, has_final):
    t = xbc.shape[0]
    q = CHUNK
    has_init = init is not None
    last8 = t // 8 - 1

    def chunk_of(s, ph, c):
        return base + s * nc + jnp.where(ph == 1, c, nc - 1 - c)

    def conv_chunk(s, ph, c):
        return base + s * nc + (1 - ph) * (nc - 1 - c)

    main = lambda s, ph, c: (conv_chunk(s, ph, c), 0)
    prev = lambda s, ph, c: (jnp.maximum(conv_chunk(s, ph, c) * (q // 8) - 1, 0), 0)
    nxt = lambda s, ph, c: (jnp.minimum((conv_chunk(s, ph, c) + 1) * (q // 8), last8), 0)
    table = lambda s, ph, c: (1 - ph, chunk_of(s, ph, c), 0)
    const = lambda s, ph, c: (0, 0)

    in_specs = [
        pl.BlockSpec((q, CONV_DIM), main),
        pl.BlockSpec((8, CONV_DIM), prev),
        pl.BlockSpec((8, CONV_DIM), nxt),
        pl.BlockSpec((1, N_HEADS, q), table),
        pl.BlockSpec((1, N_HEADS, q), table),
        pl.BlockSpec((1, q, LANES), table),
        pl.BlockSpec((q, D_SSD), lambda s, ph, c: (base + s * nc + ph * c, 0)),
    ]
    args = [xbc, xbc, xbc, ra, rw, cc, z]
    if has_init:
        in_specs.append(pl.BlockSpec((1, 1, 1, D_SSD, D_STATE), lambda s, ph, c: (s, layer, 1 - ph, 0, 0)))
        args.append(init)
    for a in (cw, cb, dsk, nw, e_mat):
        in_specs.append(pl.BlockSpec(a.shape, const))
        args.append(a)

    out_specs = [pl.BlockSpec((q, D_SSD), lambda s, ph, c: (s * nc + ph * c, 0))]
    out_shape = [jax.ShapeDtypeStruct((n_seq * nc * q, D_SSD), BF16)]
    if has_final:
        out_specs.append(pl.BlockSpec((1, 1, D_SSD, D_STATE), lambda s, ph, c: (s, 1 - ph, 0, 0)))
        out_shape.append(jax.ShapeDtypeStruct((n_seq, 2, D_SSD, D_STATE), F32))

    out = pl.pallas_call(
        functools.partial(_ssd_kernel, nc=nc, has_init=has_init, has_final=has_final),
        grid=(n_seq, 2, nc),
        in_specs=in_specs,
        out_specs=out_specs,
        out_shape=out_shape,
        scratch_shapes=[
            pltpu.VMEM((D_STATE, D_SSD), F32),
            pltpu.VMEM((nc * q, D_SSD), BF16),
            pltpu.VMEM((nc * q, D_SSD), BF16),
            pltpu.VMEM((nc * q, D_BC), BF16),
            pltpu.VMEM((nc * N_BC_GROUPS * q, q), BF16),
            pltpu.VMEM((nc * N_BC_GROUPS * q, q), BF16),
        ],
        compiler_params=pltpu.CompilerParams(dimension_semantics=("arbitrary", "arbitrary", "arbitrary")),
        name="ssd_mixer_nc%d" % nc,
    )(*args)
    return out if has_final else (out[0], None)


def _window_count(idx, w, n):
    hi = jnp.minimum(idx + (w // 2 - 1), n - 1)
    lo = jnp.maximum(idx - w // 2, 0)
    return hi - lo + 1


def _pool_tail(s, cnt, vt, g, pw_ref, ps_ref):
    cnt2 = jnp.concatenate([cnt, cnt], axis=1).astype(F32)
    upre = s / cnt2 - vt.astype(F32)
    gs = slice(g * POOL_GROUP_W, (g + 1) * POOL_GROUP_W)
    return (_dot(upre.astype(BF16), pw_ref[g]) * ps_ref[:, gs]).astype(BF16)


def _pool_seq_kernel(v_ref, band_ref, pw_ref, ps_ref, u_ref):
    n = v_ref.shape[0]
    pos = lax.broadcasted_iota(I32, (n, LANES), 0)
    for g, w in enumerate(POOL_WINDOWS):
        gs = slice(g * POOL_GROUP_W, (g + 1) * POOL_GROUP_W)
        vt = v_ref[:, gs]
        s = _dot(band_ref[g], vt)
        u_ref[:, gs] = _pool_tail(s, _window_count(pos, w, n), vt, g, pw_ref, ps_ref)


def _pool_grid_kernel(v_ref, band_ref, pw_ref, ps_ref, u_ref, p_ref, *, rows):
    tile = 256
    n = rows * GRID_W
    pad = (max(POOL_WINDOWS) // 2) * GRID_W
    p_ref[0:pad, :] = jnp.zeros((pad, POOL_GROUP_W), F32)
    p_ref[pad + n:pad + n + pad, :] = jnp.zeros((pad, POOL_GROUP_W), F32)
    for g, w in enumerate(POOL_WINDOWS):
        gs = slice(g * POOL_GROUP_W, (g + 1) * POOL_GROUP_W)

        def colsum(t, carry):
            r0 = pl.multiple_of(t * tile, tile)
            p_ref[pl.ds(pad + r0, tile), :] = _dot(band_ref[g], v_ref[pl.ds(r0, tile), gs])
            return carry

        lax.fori_loop(0, n // tile, colsum, 0)

        def rowsum(t, carry):
            r0 = pl.multiple_of(t * tile, tile)
            s = p_ref[pl.ds(pad + r0 - (w // 2) * GRID_W, tile), :]
            for d in range(-(w // 2) + 1, w // 2):
                s = s + p_ref[pl.ds(pad + r0 + d * GRID_W, tile), :]
            tok = r0 + lax.broadcasted_iota(I32, (tile, LANES), 0)
            grid_row = jnp.right_shift(tok, GRID_W.bit_length() - 1)
            grid_col = jnp.bitwise_and(tok, GRID_W - 1)
            cnt = _window_count(grid_row, w, rows) * _window_count(grid_col, w, GRID_W)
            vt = v_ref[pl.ds(r0, tile), gs]
            u_ref[pl.ds(r0, tile), gs] = _pool_tail(s, cnt, vt, g, pw_ref, ps_ref)
            return carry

        lax.fori_loop(0, n // tile, rowsum, 0)


def _band_matrix(n, w):
    idx = np.arange(n)
    lo = np.clip(idx - w // 2, 0, n - 1)
    hi = np.clip(idx + w // 2 - 1, 0, n - 1)
    j = idx[None, :]
    return ((j >= lo[:, None]) & (j <= hi[:, None])).astype(np.float32)


def _pool_call(v, pw, ps, *, n_seq, seq_len, base_block, grid_rows):
    if grid_rows is None:
        bands = np.stack([_band_matrix(seq_len, w) for w in POOL_WINDOWS])
        kern = _pool_seq_kernel
        scratch = []
    else:
        per = 256 // GRID_W
        bands = np.stack([np.kron(np.eye(per, dtype=np.float32), _band_matrix(GRID_W, w)) for w in POOL_WINDOWS])
        kern = functools.partial(_pool_grid_kernel, rows=grid_rows)
        pad = (max(POOL_WINDOWS) // 2) * GRID_W
        scratch = [pltpu.VMEM((seq_len + 2 * pad, POOL_GROUP_W), F32)]
    bands = jnp.asarray(bands, BF16)
    const3 = lambda s: (0, 0, 0)
    return pl.pallas_call(
        kern,
        grid=(n_seq,),
        in_specs=[
            pl.BlockSpec((seq_len, D_MODEL), lambda s: (base_block + s, 0)),
            pl.BlockSpec(bands.shape, const3),
            pl.BlockSpec(pw.shape, const3),
            pl.BlockSpec(ps.shape, lambda s: (0, 0)),
        ],
        out_specs=pl.BlockSpec((seq_len, D_MODEL), lambda s: (s, 0)),
        out_shape=jax.ShapeDtypeStruct((n_seq * seq_len, D_MODEL), BF16),
        scratch_shapes=scratch,
        name="pool_mixer_len%d" % seq_len,
    )(v, bands, pw, ps)


def _route_kernel(yp_ref, ys_ref, up_ref, us_ref, xp_ref, xs_ref, mod_ref, wy_ref, wu_ref, nw_ref, wr_ref, rb_ref,
                  x1_ref, h2_ref, cls_ref, *, prompt_tiles):
    tm = x1_ref.shape[0]
    is_prompt = pl.program_id(0) < prompt_tiles
    y = jnp.where(is_prompt, yp_ref[...], ys_ref[...])
    u = jnp.where(is_prompt, up_ref[...], us_ref[...])
    x = jnp.where(is_prompt, xp_ref[...], xs_ref[...])
    mix = _dot(y, wy_ref[0]) + _dot(u, wu_ref[0])
    x1 = x + mod_ref[0, 2:3, :] * mix
    x1_ref[...] = x1
    h2 = (_rms(x1) * nw_ref[...]) * (1.0 + mod_ref[0, 4:5, :]) + mod_ref[0, 3:4, :]
    _store_slabs(h2_ref, h2)

    logits = _dot_split(h2, wr_ref[...]).T[0:N_EXPERTS, :]
    e = jnp.exp(logits - jnp.max(logits, axis=0, keepdims=True))
    probs = e / jnp.sum(e, axis=0, keepdims=True)
    sel = probs + rb_ref[...]
    s_rows = [sel[i:i + 1, :] for i in range(N_EXPERTS)]

    scores = []
    for g in range(N_EXPERT_GROUPS):
        a, b, c, d = s_rows[4 * g:4 * g + 4]
        hi1, lo1 = jnp.maximum(a, b), jnp.minimum(a, b)
        hi2, lo2 = jnp.maximum(c, d), jnp.minimum(c, d)
        scores.append(jnp.maximum(hi1, hi2) + jnp.maximum(jnp.minimum(hi1, hi2), jnp.maximum(lo1, lo2)))
    best_g = jnp.zeros((1, tm), I32)
    best_s = scores[0]
    for g in range(1, N_EXPERT_GROUPS):
        better = scores[g] > best_s
        best_g = jnp.where(better, g, best_g)
        best_s = jnp.where(better, scores[g], best_s)

    def in_group(rows, j):
        out = rows[j]
        for g in range(1, N_EXPERT_GROUPS):
            out = jnp.where(best_g == g, rows[4 * g + j], out)
        return out

    sg = [in_group(s_rows, j) for j in range(EXPERTS_PER_GROUP)]
    i1 = jnp.zeros((1, tm), I32)
    v1 = sg[0]
    for j in range(1, EXPERTS_PER_GROUP):
        better = sg[j] > v1
        i1 = jnp.where(better, j, i1)
        v1 = jnp.where(better, sg[j], v1)
    i2 = jnp.zeros((1, tm), I32)
    v2 = jnp.where(i1 == 0, NEG_INF, sg[0])
    for j in range(1, EXPERTS_PER_GROUP):
        cand = jnp.where(i1 == j, NEG_INF, sg[j])
        better = cand > v2
        i2 = jnp.where(better, j, i2)
        v2 = jnp.where(better, cand, v2)

    lo = jnp.minimum(i1, i2)
    hi = jnp.maximum(i1, i2)
    pair = jnp.where(lo == 0, hi - 1, jnp.where(lo == 1, hi + 1, 5))
    cls_ref[0] = best_g * len(EXPERT_PAIRS) + pair


def _route_call(yp, ys, up, us, xp, xs, mod, w_out, layer, nw, wr, rb, mod_row):
    t_p, t_s = xp.shape[0], xs.shape[0]
    t = t_p + t_s
    tm = TM_TOK
    row = lambda i: (i, 0)
    const = lambda i: (0, 0)
    return pl.pallas_call(
        functools.partial(_route_kernel, prompt_tiles=t_p // tm),
        grid=(t // tm,),
        in_specs=_group_specs(tm, D_SSD, t_p, t_s) + _group_specs(tm, D_MODEL, t_p, t_s)
        + _group_specs(tm, D_MODEL, t_p, t_s) + [
            pl.BlockSpec((1, 8, D_MODEL), lambda i: (mod_row(i * tm), 0, 0)),
            pl.BlockSpec((1, D_SSD, D_MODEL), lambda i: (layer, 0, 0)),
            pl.BlockSpec((1, D_MODEL, D_MODEL), lambda i: (layer, D_SSD // D_MODEL, 0)),
            pl.BlockSpec((1, D_MODEL), const),
            pl.BlockSpec(wr.shape, const),
            pl.BlockSpec(rb.shape, const),
        ],
        out_specs=[
            pl.BlockSpec((tm, D_MODEL), row),
            pl.BlockSpec((tm * SLAB, LANES), row),
            pl.BlockSpec((1, 1, tm), lambda i: (i, 0, 0)),
        ],
        out_shape=[
            jax.ShapeDtypeStruct((t, D_MODEL), F32),
            jax.ShapeDtypeStruct((t * SLAB, LANES), F32),
            jax.ShapeDtypeStruct((t // tm, 1, tm), I32),
        ],
        name="out_proj_router",
    )(yp, ys, up, us, xp, xs, mod, w_out, w_out, nw, wr, rb)


def _sort_kernel(cls_ref, pos_ref, tile_ref):
    cls = cls_ref[...]
    nr = cls.shape[0]
    ci = lax.broadcasted_iota(I32, (LANES, LANES), 0)
    cj = lax.broadcasted_iota(I32, (LANES, LANES), 1)
    upper = jnp.where(ci <= cj, 1.0, 0.0).astype(BF16)
    ri = lax.broadcasted_iota(I32, (nr, nr), 0)
    rj = lax.broadcasted_iota(I32, (nr, nr), 1)
    lower = jnp.where(rj < ri, 1.0, 0.0).astype(BF16)
    tile_idx = lax.broadcasted_iota(I32, (8, LANES), 1).astype(F32)
    pos = jnp.zeros(cls.shape, F32)
    tile_cls = jnp.zeros((8, LANES), F32)
    tiles_done = jnp.zeros((1, 1), F32)
    for k in range(N_CLASSES):
        member = cls == k
        mf = jnp.where(member, 1.0, 0.0)
        within = _dot(mf.astype(BF16), upper)
        row_tot = jnp.broadcast_to(within[:, LANES - 1:LANES], cls.shape)
        before = _dot(lower, row_tot.astype(BF16))
        rank = within - mf + before
        count = jnp.sum(jnp.sum(mf, axis=1, keepdims=True), axis=0, keepdims=True)
        n_tiles = jnp.floor((count + (TM_EXP - 1)) * (1.0 / TM_EXP))
        pos = pos + jnp.where(member, tiles_done * TM_EXP + rank, 0.0)
        tiles_done = tiles_done + n_tiles
        tile_cls = tile_cls + jnp.where(tile_idx >= tiles_done, 1.0, 0.0)
    pos_ref[...] = pos.astype(I32)
    r = lax.broadcasted_iota(I32, (8, LANES), 0)
    tile_ref[...] = jnp.where(r == 1, tiles_done, tile_cls).astype(I32)


def _sort_call(cls):
    t = cls.size
    rows = -(-t // (LANES * LANES)) * LANES
    cls2 = jnp.pad(cls.reshape(t // LANES, LANES), ((0, rows - t // LANES), (0, 0)), constant_values=N_CLASSES)
    pos, tiles = pl.pallas_call(
        _sort_kernel,
        out_shape=[jax.ShapeDtypeStruct(cls2.shape, I32), jax.ShapeDtypeStruct((8, LANES), I32)],
        name="route_sort",
    )(cls2)
    return pos.reshape(-1)[:t], tiles


def _slab(ref, tok):
    return ref.at[pl.ds(pl.multiple_of(tok * SLAB, SLAB), SLAB), :]


def _scatter_kernel(pos_ref, src_ref, dst_in, dst_ref, sem):
    del dst_in
    tm = src_ref.shape[0] // SLAB
    base = pl.program_id(0) * tm

    def row_copy(r):
        return pltpu.make_async_copy(_slab(src_ref, r), _slab(dst_ref, pos_ref[base + r]), sem)

    def body(i, carry):
        for p in range(DMA_THREADS):
            row_copy(i * DMA_THREADS + p).start(priority=p)
        return carry

    lax.fori_loop(0, tm // DMA_THREADS, body, 0, unroll=DMA_UNROLL // DMA_THREADS)

    def drain(r, carry):
        row_copy(r).wait()
        return carry

    lax.fori_loop(0, tm, drain, 0, unroll=DMA_UNROLL)


def _scatter_call(pos, src, n_tokens):
    tm = TM_TOK
    t = src.shape[0] // SLAB
    dst = jnp.zeros((n_tokens * SLAB, LANES), src.dtype)
    return pl.pallas_call(
        _scatter_kernel,
        grid_spec=pltpu.PrefetchScalarGridSpec(
            num_scalar_prefetch=1,
            grid=(t // tm,),
            in_specs=[pl.BlockSpec((tm * SLAB, LANES), lambda i, pos: (i, 0)),
                      pl.BlockSpec(memory_space=pl.ANY)],
            out_specs=pl.BlockSpec(memory_space=pl.ANY),
            scratch_shapes=[pltpu.SemaphoreType.DMA(())],
        ),
        out_shape=jax.ShapeDtypeStruct(dst.shape, dst.dtype),
        input_output_aliases={2: 0},
        name="route_scatter",
    )(pos, src, dst)


def _unpermute_kernel(pos_ref, x1_ref, mod_ref, nw_ref, y_hbm, op_ref, os_ref, buf, sem, *, final_norm,
                      prompt_tiles):
    tm = x1_ref.shape[0]
    base = pl.program_id(0) * tm

    def row_copy(r):
        return pltpu.make_async_copy(_slab(y_hbm, pos_ref[base + r]), _slab(buf, r), sem)

    def body(i, carry):
        for p in range(DMA_THREADS):
            row_copy(i * DMA_THREADS + p).start(priority=p)
        return carry

    lax.fori_loop(0, tm // DMA_THREADS, body, 0, unroll=DMA_UNROLL // DMA_THREADS)

    def drain(r, carry):
        row_copy(r).wait()
        return carry

    lax.fori_loop(0, tm, drain, 0, unroll=DMA_UNROLL)
    x2 = x1_ref[...] + mod_ref[0, 5:6, :] * _load_slabs(buf, tm)
    if final_norm:
        x2 = _rms(x2) * nw_ref[...]

    @pl.when(pl.program_id(0) < prompt_tiles)
    def _():
        op_ref[...] = x2

    @pl.when(pl.program_id(0) >= prompt_tiles)
    def _():
        os_ref[...] = x2


def _unpermute_call(pos, x1, mod, nw, y_sorted, mod_row, t_p, final_norm):
    t = x1.shape[0]
    tm = TM_TOK
    return pl.pallas_call(
        functools.partial(_unpermute_kernel, final_norm=final_norm, prompt_tiles=t_p // tm),
        grid_spec=pltpu.PrefetchScalarGridSpec(
            num_scalar_prefetch=1,
            grid=(t // tm,),
            in_specs=[pl.BlockSpec((tm, D_MODEL), lambda i, pos: (i, 0)),
                      pl.BlockSpec((1, 8, D_MODEL), lambda i, pos: (mod_row(i * tm), 0, 0)),
                      pl.BlockSpec((1, D_MODEL), lambda i, pos: (0, 0)),
                      pl.BlockSpec(memory_space=pl.ANY)],
            out_specs=_group_specs(tm, D_MODEL, t_p, t - t_p),
            scratch_shapes=[pltpu.VMEM((tm * SLAB, LANES), F32), pltpu.SemaphoreType.DMA(())],
        ),
        out_shape=[jax.ShapeDtypeStruct((t_p, D_MODEL), F32), jax.ShapeDtypeStruct((t - t_p, D_MODEL), F32)],
        name="moe_unpermute",
    )(pos, x1, mod, nw, y_sorted)


def _moe_kernel(ea_ref, eb_ref, nact_ref, x_ref, wr_ref, wga, wua, wda, wgb, wub, wdb, y_ref):
    j = pl.program_id(0)
    tm = x_ref.shape[0] // SLAB

    @pl.when(j < nact_ref[0])
    def _():
        x = _load_slabs(x_ref, tm).astype(BF16)
        logits = _dot(x, wr_ref[...])
        lane = lax.broadcasted_iota(I32, logits.shape, 1)
        l_a = jnp.sum(jnp.where(lane == ea_ref[j], logits, 0.0), axis=1, keepdims=True)
        l_b = jnp.sum(jnp.where(lane == eb_ref[j], logits, 0.0), axis=1, keepdims=True)
        gate_a = 1.0 / (1.0 + jnp.exp(l_b - l_a))
        gate_b = 1.0 / (1.0 + jnp.exp(l_a - l_b))

        def ffn(wg, wu, wd, gate):
            g = _dot(x, wg[0, 0].astype(BF16))
            u = _dot(x, wu[0, 0].astype(BF16))
            a = _silu(g) * u * gate
            return _dot(a.astype(BF16), wd[0, 0].astype(BF16))

        _store_slabs(y_ref, ffn(wga, wua, wda, gate_a) + ffn(wgb, wub, wdb, gate_b))

    @pl.when(j >= nact_ref[0])
    def _():
        y_ref[...] = jnp.zeros_like(y_ref)


def _moe_call(ea, eb, nact, xs, wr, w_gate, w_up, w_down, layer):
    n_rows = xs.shape[0] // SLAB
    tm = TM_EXP
    wa = lambda j, ea, eb, na: (layer, ea[j], 0, 0)
    wb = lambda j, ea, eb, na: (layer, eb[j], 0, 0)
    gu = (1, 1, D_MODEL, D_FF)
    dn = (1, 1, D_FF, D_MODEL)
    tile = pl.BlockSpec((tm * SLAB, LANES), lambda j, ea, eb, na: (j, 0))
    return pl.pallas_call(
        _moe_kernel,
        grid_spec=pltpu.PrefetchScalarGridSpec(
            num_scalar_prefetch=3,
            grid=(n_rows // tm,),
            in_specs=[tile, pl.BlockSpec(wr.shape, lambda j, ea, eb, na: (0, 0)),
                      pl.BlockSpec(gu, wa), pl.BlockSpec(gu, wa), pl.BlockSpec(dn, wa),
                      pl.BlockSpec(gu, wb), pl.BlockSpec(gu, wb), pl.BlockSpec(dn, wb)],
            out_specs=tile,
        ),
        out_shape=jax.ShapeDtypeStruct((n_rows * SLAB, LANES), F32),
        name="moe_ffn",
    )(ea, eb, nact, xs, wr, w_gate, w_up, w_down, w_gate, w_up, w_down)


def kernel(x_prompt, x_sample, state_ssm, c, c_ctx, w_ada, b_ada, norm1_w, w_in, conv_w, conv_b, dt_bias, a_log, d_skip, ssd_norm_w, pool_w, pool_scale, w_out, norm2_w, w_router, router_bias, w_gate, w_up, w_down, final_norm_w):
    n_p, len_p, d = x_prompt.shape
    n_s, len_s, _ = x_sample.shape
    depth = w_ada.shape[0]
    t_p, t_s = n_p * len_p, n_s * len_s
    t = t_p + t_s
    assert d == D_MODEL and len_p % TM_TOK == 0 and len_s % TM_PROJ == 0 and t_p % TM_PROJ == 0
    assert len_s % GRID_W == 0 and t_p % len_s == 0

    def mod_row(tok):
        return jnp.where(tok < t_p, 0, 1 + (tok - t_p) // len_s)

    xp, xs = x_prompt.reshape(t_p, d), x_sample.reshape(t_s, d)

    cvec = jnp.zeros((8, d), F32).at[0].set(c_ctx).at[1:1 + n_s].set(c)
    ada = _ada_call(cvec, w_ada, b_ada)
    mods = jnp.pad(ada[:, :1 + n_s].reshape(depth, 1 + n_s, 6, d), ((0, 0), (0, 0), (0, 2), (0, 0)))

    e_np = np.zeros((LANES, D_SSD), np.float32)
    e_np[N_HEADS:2 * N_HEADS] = np.kron(np.eye(N_HEADS, dtype=np.float32), np.ones((1, HEADDIM), np.float32))
    e_mat = jnp.asarray(e_np, BF16)
    wr = jnp.pad(w_router, ((0, 0), (0, LANES - N_EXPERTS)))
    rb = jnp.broadcast_to(router_bias[:, None], (N_EXPERTS, TM_TOK))
    pairs = np.asarray(EXPERT_PAIRS, np.int32)
    n_sorted = t + N_CLASSES * TM_EXP
    n_tiles = n_sorted // TM_EXP
    assert n_tiles <= LANES

    init_all = state_ssm.reshape(n_s, depth, 2, D_SSD, D_STATE)
    w_out_b = w_out.astype(BF16)

    new_states = []
    for l in range(depth):
        w_l = w_in[l]
        wz = w_l[:, :D_SSD].astype(BF16)
        wx = w_l[:, D_SSD:D_SSD + CONV_DIM].astype(BF16)
        o = D_SSD + CONV_DIM
        wd = jnp.pad(w_l[:, o:o + 2 * N_HEADS], ((0, 0), (0, LANES - 2 * N_HEADS))).astype(BF16)
        wv = w_l[:, o + 2 * N_HEADS:].astype(BF16)
        z, xbc, v, dt = _inproj_call(xp, xs, mods[l], norm1_w[l][None], wz, wx, wv, wd, mod_row)

        dtb = jnp.broadcast_to(dt_bias[l].reshape(2 * N_HEADS, 1), (2 * N_HEADS, CHUNK))
        alog = jnp.broadcast_to(a_log[l].reshape(2 * N_HEADS, 1), (2 * N_HEADS, CHUNK))
        dsk = jnp.repeat(d_skip[l], HEADDIM, axis=1)
        ra, rw, cc = _decay_call(dt, dtb, alog)
        ssd_args = (conv_w[l], conv_b[l][None], dsk, ssd_norm_w[l][None], e_mat)
        y_p, fin = _ssd_call(xbc, ra, rw, cc, z, None, l, *ssd_args, n_seq=n_p, nc=len_p // CHUNK, base=0,
                             has_final=True)
        y_s, _ = _ssd_call(xbc, ra, rw, cc, z, init_all, l, *ssd_args, n_seq=n_s, nc=len_s // CHUNK,
                           base=t_p // CHUNK, has_final=False)
        new_states.append(fin)

        pw = pool_w[l].astype(BF16)
        ps = pool_scale[l][None]
        u_p = _pool_call(v, pw, ps, n_seq=n_p, seq_len=len_p, base_block=0, grid_rows=None)
        u_s = _pool_call(v, pw, ps, n_seq=n_s, seq_len=len_s, base_block=t_p // len_s, grid_rows=len_s // GRID_W)

        x1, h2, cls = _route_call(y_p, y_s, u_p, u_s, xp, xs, mods[l], w_out_b, l, norm2_w[l][None], wr, rb, mod_row)

        pos, tiles = _sort_call(cls)
        nact = tiles[1, 0:1]
        tcls = tiles[0, :n_tiles]
        tcls = jnp.minimum(tcls, tcls[jnp.maximum(nact[0] - 1, 0)])
        tcls = jnp.minimum(tcls, N_CLASSES - 1)
        grp = tcls // len(EXPERT_PAIRS)
        pr = tcls % len(EXPERT_PAIRS)
        ea = grp * EXPERTS_PER_GROUP + jnp.asarray(pairs[:, 0])[pr]
        eb = grp * EXPERTS_PER_GROUP + jnp.asarray(pairs[:, 1])[pr]

        h2s = _scatter_call(pos, h2, n_sorted)
        y_sorted = _moe_call(ea, eb, nact, h2s, wr.astype(BF16), w_gate, w_up, w_down, l)
        last = l == depth - 1
        xp, xs = _unpermute_call(pos, x1, mods[l], final_norm_w[None], y_sorted, mod_row, t_p, last)

    y_prompt = xp.reshape(n_p, len_p, d)
    y_sample = xs.reshape(n_s, len_s, d)
    ns = jnp.stack(new_states, axis=1)
    new_state_ssm = ns.reshape(n_p, depth, 2, N_HEADS, HEADDIM, D_STATE)
    return (y_prompt, y_sample, new_state_ssm)
```

```python
import functools

import numpy as np
import jax
import jax.numpy as jnp
from jax import lax
from jax.experimental import pallas as pl
from jax.experimental.pallas import tpu as pltpu

F32 = jnp.float32
BF16 = jnp.bfloat16
I32 = jnp.int32

D_MODEL = 1024
HEADDIM = 64
N_HEADS = 16
N_BC_GROUPS = 2
HEADS_PER_BC = N_HEADS // N_BC_GROUPS
D_STATE = 128
CHUNK = 128
D_SSD = N_HEADS * HEADDIM
D_BC = N_BC_GROUPS * D_STATE
CONV_DIM = D_SSD + 2 * D_BC
POOL_WINDOWS = (2, 4, 8, 16)
POOL_GROUP_W = 256
GRID_W = 64
N_EXPERTS = 16
EXPERTS_PER_GROUP = 4
N_EXPERT_GROUPS = N_EXPERTS // EXPERTS_PER_GROUP
D_FF = 512
EPS = 1e-6

LANES = 128
HEAD_QUAD = 4
QUAD_W = HEAD_QUAD * HEADDIM
EXPERT_PAIRS = ((0, 1), (0, 2), (0, 3), (1, 2), (1, 3), (2, 3))
N_CLASSES = N_EXPERT_GROUPS * len(EXPERT_PAIRS)

TM_PROJ = 512
TM_TOK = 256
TM_EXP = 256
SLAB = D_MODEL // LANES
DMA_UNROLL = 8
DMA_THREADS = 2
NEG_INF = float("-inf")


def _dot(a, b):
    return jnp.dot(a, b, preferred_element_type=F32)


def _split_bf16(x):
    hi = x.astype(BF16)
    lo = (x - hi.astype(F32)).astype(BF16)
    return hi, lo


def _dot_split(a, w):
    ah, al = _split_bf16(a)
    wh, wl = _split_bf16(w)
    return _dot(ah, wh) + (_dot(al, wh) + _dot(ah, wl))


def _silu(x):
    return x / (1.0 + jnp.exp(-x))


def _rms(x):
    return x * lax.rsqrt(jnp.mean(x * x, axis=-1, keepdims=True) + EPS)


def _store_slabs(ref, val):
    n = val.shape[0]
    for k in range(SLAB):
        ref[pl.ds(k, n, stride=SLAB), :] = val[:, k * LANES:(k + 1) * LANES]


def _load_slabs(ref, n):
    return jnp.concatenate([ref[pl.ds(k, n, stride=SLAB), :] for k in range(SLAB)], axis=1)


def _ada_kernel(c_ref, w_ref, b_ref, o_ref):
    o_ref[0] = _dot_split(_silu(c_ref[...]), w_ref[0]) + b_ref[0]


def _ada_call(cvec, w_ada, b_ada):
    depth, d, n = w_ada.shape
    tn = 1024
    return pl.pallas_call(
        _ada_kernel,
        grid=(depth, n // tn),
        in_specs=[
            pl.BlockSpec((8, d), lambda l, j: (0, 0)),
            pl.BlockSpec((1, d, tn), lambda l, j: (l, 0, j)),
            pl.BlockSpec((1, 1, tn), lambda l, j: (l, 0, j)),
        ],
        out_specs=pl.BlockSpec((1, 8, tn), lambda l, j: (l, 0, j)),
        out_shape=jax.ShapeDtypeStruct((depth, 8, n), F32),
        name="ada_ln",
    )(cvec, w_ada, b_ada.reshape(depth, 1, n))


def _group_specs(tm, width, p_rows, s_rows):
    pt, st = p_rows // tm, s_rows // tm
    return [pl.BlockSpec((tm, width), lambda i, *_: (jnp.minimum(i, pt - 1), 0)),
            pl.BlockSpec((tm, width), lambda i, *_: (jnp.clip(i - pt, 0, st - 1), 0))]


def _inproj_kernel(xp_ref, xs_ref, mod_ref, nw_ref, wz_ref, wx_ref, wv_ref, wd_ref,
                   z_ref, xbc_ref, v_ref, dt_ref, *, prompt_tiles):
    x = jnp.where(pl.program_id(0) < prompt_tiles, xp_ref[...], xs_ref[...])
    h = _rms(x) * nw_ref[...]
    h = h * (1.0 + mod_ref[0, 1:2, :]) + mod_ref[0, 0:1, :]
    hb = h.astype(BF16)
    z_ref[...] = _dot(hb, wz_ref[...]).astype(BF16)
    xbc_ref[...] = _dot(hb, wx_ref[...])
    v_ref[...] = _dot(hb, wv_ref[...]).astype(BF16)
    dt_ref[...] = _dot(hb, wd_ref[...])


def _inproj_call(xp, xs, mod, nw, wz, wx, wv, wd, mod_row):
    t = xp.shape[0] + xs.shape[0]
    tm = TM_PROJ
    const = lambda i: (0, 0)
    row = lambda i: (i, 0)
    return pl.pallas_call(
        functools.partial(_inproj_kernel, prompt_tiles=xp.shape[0] // tm),
        grid=(t // tm,),
        in_specs=_group_specs(tm, D_MODEL, xp.shape[0], xs.shape[0]) + [
            pl.BlockSpec((1, 8, D_MODEL), lambda i: (mod_row(i * tm), 0, 0)),
            pl.BlockSpec((1, D_MODEL), const),
            pl.BlockSpec(wz.shape, const),
            pl.BlockSpec(wx.shape, const),
            pl.BlockSpec(wv.shape, const),
            pl.BlockSpec(wd.shape, const),
        ],
        out_specs=[
            pl.BlockSpec((tm, D_SSD), row),
            pl.BlockSpec((tm, CONV_DIM), row),
            pl.BlockSpec((tm, D_MODEL), row),
            pl.BlockSpec((tm, LANES), row),
        ],
        out_shape=[
            jax.ShapeDtypeStruct((t, D_SSD), BF16),
            jax.ShapeDtypeStruct((t, CONV_DIM), F32),
            jax.ShapeDtypeStruct((t, D_MODEL), BF16),
            jax.ShapeDtypeStruct((t, LANES), F32),
        ],
        name="in_proj",
    )(xp, xs, mod, nw, wz, wx, wv, wd)


LOG2E = 1.4426950408889634
PREP_CHUNKS = 8


def _decay_kernel(dt_ref, dtb_ref, alog_ref, ra_ref, rw_ref, cc_ref):
    q = CHUNK
    ci = lax.broadcasted_iota(I32, (q, q), 0)
    cj = lax.broadcasted_iota(I32, (q, q), 1)
    upper = jnp.where(ci <= cj, 1.0, 0.0).astype(BF16)
    t_parts = [dt_ref[k * q:(k + 1) * q, :].T for k in range(PREP_CHUNKS)]
    for d in range(2):
        x = jnp.concatenate([t[d * N_HEADS:(d + 1) * N_HEADS, :] for t in t_parts], axis=0)
        tile = lambda r: jnp.concatenate([r[d * N_HEADS:(d + 1) * N_HEADS, :]] * PREP_CHUNKS, axis=0)
        xdt = x + tile(dtb_ref)
        dtv = jnp.maximum(xdt, 0.0) + jnp.log(1.0 + jnp.exp(-jnp.abs(xdt)))
        a_step = dtv * (-jnp.exp(tile(alog_ref)))
        hi = a_step.astype(BF16)
        r1 = a_step - hi.astype(F32)
        mid = r1.astype(BF16)
        lo = (r1 - mid.astype(F32)).astype(BF16)
        pref = _dot(hi, upper) + (_dot(mid, upper) + _dot(lo, upper))
        total = pref[:, q - 1:q]
        cum = pref if d == 0 else total - pref + a_step
        ra_ref[d] = (cum - jnp.log(dtv)) * LOG2E
        rw_ref[d] = dtv * jnp.exp(total - cum)
        cum2 = cum * LOG2E
        e_cum = jnp.exp(cum)
        pad = jnp.zeros((q - 2 * N_HEADS, q), F32)
        for k in range(PREP_CHUNKS):
            rs = slice(k * N_HEADS, (k + 1) * N_HEADS)
            cc_ref[d, k * q:(k + 1) * q, :] = jnp.concatenate([cum2[rs], e_cum[rs], pad], axis=0).T


def _decay_call(dt, dtb, alog):
    t = dt.shape[0]
    q = CHUNK
    tm = PREP_CHUNKS * q
    rows = PREP_CHUNKS * N_HEADS
    n_rows = t // q * N_HEADS
    const = lambda i: (0, 0)
    return pl.pallas_call(
        _decay_kernel,
        grid=(t // tm,),
        in_specs=[pl.BlockSpec((tm, LANES), lambda i: (i, 0)),
                  pl.BlockSpec(dtb.shape, const),
                  pl.BlockSpec(alog.shape, const)],
        out_specs=[pl.BlockSpec((2, rows, q), lambda i: (0, i, 0)),
                   pl.BlockSpec((2, rows, q), lambda i: (0, i, 0)),
                   pl.BlockSpec((2, tm, LANES), lambda i: (0, i, 0))],
        out_shape=[jax.ShapeDtypeStruct((2, n_rows, q), F32),
                   jax.ShapeDtypeStruct((2, n_rows, q), F32),
                   jax.ShapeDtypeStruct((2, t, LANES), F32)],
        name="ssd_decay_tables",
    )(dt, dtb, alog)


EXP2_CLAMP = 100.0


def _ssd_kernel(*refs, nc, has_init, has_final):
    it = iter(refs)
    xm_ref, xp_ref, xn_ref, ra_ref, rw_ref, cc_ref, z_ref = (next(it) for _ in range(7))
    init_ref = next(it) if has_init else None
    cw_ref, cb_ref, dsk_ref, nw_ref, e_ref = (next(it) for _ in range(5))
    y_ref = next(it)
    fin_ref = next(it) if has_final else None
    st_ref, yb_ref, xs_c, c_c, bt_c, g_c = (next(it) for _ in range(6))

    ph = pl.program_id(1)
    c = pl.program_id(2)
    fwd = ph == 1
    cc = jnp.where(fwd, c, nc - 1 - c)
    q = CHUNK
    r0 = pl.multiple_of(cc * q, q)
    r0g = pl.multiple_of(cc * (N_BC_GROUPS * q), N_BC_GROUPS * q)

    @pl.when(c == 0)
    def _():
        if has_init:
            st_ref[...] = init_ref[0, 0, 0].T
        else:
            st_ref[...] = jnp.zeros_like(st_ref)

    @pl.when(ph == 0)
    def _():
        xm = xm_ref[...]
        prev_row = jnp.where(cc > 0, xp_ref[7:8, :], 0.0)
        next0 = jnp.where(cc < nc - 1, xn_ref[0:1, :], 0.0)
        next1 = jnp.where(cc < nc - 1, xn_ref[1:2, :], 0.0)
        rows = lax.broadcasted_iota(I32, (q, 1), 0)
        x_m1 = jnp.where(rows == 0, prev_row, pltpu.roll(xm, 1, 0))
        x_p1 = jnp.where(rows == q - 1, next0, pltpu.roll(xm, q - 1, 0))
        x_p2 = jnp.where(rows == q - 2, next0, jnp.where(rows == q - 1, next1, pltpu.roll(xm, q - 2, 0)))
        conv = (cw_ref[0:1, :] * x_m1 + cw_ref[1:2, :] * xm + cw_ref[2:3, :] * x_p1
                + cw_ref[3:4, :] * x_p2 + cb_ref[...])
        act = _silu(conv)
        xs_c[pl.ds(r0, q), :] = act[:, :D_SSD].astype(BF16)
        c_c[pl.ds(r0, q), :] = act[:, D_SSD + D_BC:].astype(BF16)
        for g in range(N_BC_GROUPS):
            b_g = act[:, D_SSD + g * D_STATE:D_SSD + (g + 1) * D_STATE]
            c_g = act[:, D_SSD + D_BC + g * D_STATE:D_SSD + D_BC + (g + 1) * D_STATE].astype(BF16)
            bt_c[pl.ds(r0g + g * q, q), :] = b_g.T.astype(BF16)
            g_c[pl.ds(r0g + g * q, q), :] = lax.dot_general(
                c_g, b_g.astype(BF16), (((1,), (1,)), ((), ())), preferred_element_type=F32).astype(BF16)

    xs_b = xs_c[pl.ds(r0, q), :]
    c_all = c_c[pl.ds(r0, q), :]
    c_parts = [c_all[:, g * D_STATE:(g + 1) * D_STATE] for g in range(N_BC_GROUPS)]
    bt_parts = [bt_c[pl.ds(r0g + g * q, q), :] for g in range(N_BC_GROUPS)]

    li = lax.broadcasted_iota(I32, (q, q), 0)
    si = lax.broadcasted_iota(I32, (q, q), 1)
    dls = li - si
    causal = jnp.where(fwd, dls, -dls) >= 0
    g_parts = [jnp.where(causal, g_c[pl.ds(r0g + g * q, q), :], jnp.zeros((q, q), BF16))
               for g in range(N_BC_GROUPS)]

    src_exp = ra_ref[0]
    w_t = rw_ref[0].astype(BF16)
    col = cc_ref[0]

    lane_q = lax.broadcasted_iota(I32, (q, QUAD_W), 1) // HEADDIM
    y_parts, s_parts = [], []
    for qd in range(N_HEADS // HEAD_QUAD):
        top, bot = [], []
        for j in range(HEAD_QUAD):
            h = qd * HEAD_QUAD + j
            g = h // HEADS_PER_BC
            seg = jnp.minimum(col[:, h:h + 1] - src_exp[h:h + 1, :], EXP2_CLAMP)
            top.append(jnp.exp2(seg).astype(BF16) * g_parts[g])
            bot.append(bt_parts[g] * w_t[h:h + 1, :])
        lhs = jnp.concatenate([jnp.concatenate(top, axis=1), jnp.concatenate(bot, axis=1)], axis=0)
        xq = xs_b[:, qd * QUAD_W:(qd + 1) * QUAD_W]
        rhs = jnp.concatenate([jnp.where(lane_q == j, xq, jnp.zeros_like(xq)) for j in range(HEAD_QUAD)], axis=0)
        res = _dot(lhs, rhs)
        y_parts.append(res[:q])
        s_parts.append(res[q:])
    y_diag = jnp.concatenate(y_parts, axis=1)
    st_new = jnp.concatenate(s_parts, axis=1)

    st = st_ref[...]
    st_b = st.astype(BF16)
    half = D_SSD // N_BC_GROUPS
    y_off = jnp.concatenate([_dot(c_parts[g], st_b[:, g * half:(g + 1) * half]) for g in range(N_BC_GROUPS)],
                            axis=1)
    col_hi, col_lo = _split_bf16(col)
    e_cum = _dot(col_hi, e_ref[...]) + _dot(col_lo, e_ref[...])
    y_dir = y_diag + y_off * e_cum
    e_tot = jnp.where(fwd, e_cum[q - 1:q, :], e_cum[0:1, :])
    st_next = st * e_tot + st_new
    st_ref[...] = st_next

    if has_final:
        @pl.when(c == nc - 1)
        def _():
            fin_ref[0, 0] = st_next.T

    @pl.when(ph == 0)
    def _():
        yb_ref[pl.ds(r0, q), :] = y_dir.astype(BF16)

    @pl.when(ph == 1)
    def _():
        skip = xs_b.astype(F32) * (dsk_ref[0:1, :] + dsk_ref[1:2, :])
        y = y_dir + yb_ref[pl.ds(r0, q), :].astype(F32) + skip
        y = y * _silu(z_ref[...].astype(F32))
        y_ref[...] = (_rms(y) * nw_ref[...]).astype(BF16)


def _ssd_call(xbc, ra, rw, cc, z, init, layer, cw, cb, dsk, nw, e_mat, *, n_seq, nc, base, has_final):
    t = xbc.shape[0]
    q = CHUNK
    has_init = init is not None
    last8 = t // 8 - 1

    def chunk_of(s, ph, c):
        return base + s * nc + jnp.where(ph == 1, c, nc - 1 - c)

    def conv_chunk(s, ph, c):
        return base + s * nc + (1 - ph) * (nc - 1 - c)

    main = lambda s, ph, c: (conv_chunk(s, ph, c), 0)
    prev = lambda s, ph, c: (jnp.maximum(conv_chunk(s, ph, c) * (q // 8) - 1, 0), 0)
    nxt = lambda s, ph, c: (jnp.minimum((conv_chunk(s, ph, c) + 1) * (q // 8), last8), 0)
    table = lambda s, ph, c: (1 - ph, chunk_of(s, ph, c), 0)
    const = lambda s, ph, c: (0, 0)

    in_specs = [
        pl.BlockSpec((q, CONV_DIM), main),
        pl.BlockSpec((8, CONV_DIM), prev),
        pl.BlockSpec((8, CONV_DIM), nxt),
        pl.BlockSpec((1, N_HEADS, q), table),
        pl.BlockSpec((1, N_HEADS, q), table),
        pl.BlockSpec((1, q, LANES), table),
        pl.BlockSpec((q, D_SSD), lambda s, ph, c: (base + s * nc + ph * c, 0)),
    ]
    args = [xbc, xbc, xbc, ra, rw, cc, z]
    if has_init:
        in_specs.append(pl.BlockSpec((1, 1, 1, D_SSD, D_STATE), lambda s, ph, c: (s, layer, 1 - ph, 0, 0)))
        args.append(init)
    for a in (cw, cb, dsk, nw, e_mat):
        in_specs.append(pl.BlockSpec(a.shape, const))
        args.append(a)

    out_specs = [pl.BlockSpec((q, D_SSD), lambda s, ph, c: (s * nc + ph * c, 0))]
    out_shape = [jax.ShapeDtypeStruct((n_seq * nc * q, D_SSD), BF16)]
    if has_final:
        out_specs.append(pl.BlockSpec((1, 1, D_SSD, D_STATE), lambda s, ph, c: (s, 1 - ph, 0, 0)))
        out_shape.append(jax.ShapeDtypeStruct((n_seq, 2, D_SSD, D_STATE), F32))

    out = pl.pallas_call(
        functools.partial(_ssd_kernel, nc=nc, has_init=has_init, has_final=has_final),
        grid=(n_seq, 2, nc),
        in_specs=in_specs,
        out_specs=out_specs,
        out_shape=out_shape,
        scratch_shapes=[
            pltpu.VMEM((D_STATE, D_SSD), F32),
            pltpu.VMEM((nc * q, D_SSD), BF16),
            pltpu.VMEM((nc * q, D_SSD), BF16),
            pltpu.VMEM((nc * q, D_BC), BF16),
            pltpu.VMEM((nc * N_BC_GROUPS * q, q), BF16),
            pltpu.VMEM((nc * N_BC_GROUPS * q, q), BF16),
        ],
        compiler_params=pltpu.CompilerParams(dimension_semantics=("arbitrary", "arbitrary", "arbitrary")),
        name="ssd_mixer_nc%d" % nc,
    )(*args)
    return out if has_final else (out[0], None)


def _window_count(idx, w, n):
    hi = jnp.minimum(idx + (w // 2 - 1), n - 1)
    lo = jnp.maximum(idx - w // 2, 0)
    return hi - lo + 1


def _pool_tail(s, cnt, vt, g, pw_ref, ps_ref):
    cnt2 = jnp.concatenate([cnt, cnt], axis=1).astype(F32)
    upre = s / cnt2 - vt.astype(F32)
    gs = slice(g * POOL_GROUP_W, (g + 1) * POOL_GROUP_W)
    return (_dot(upre.astype(BF16), pw_ref[g]) * ps_ref[:, gs]).astype(BF16)


def _pool_seq_kernel(v_ref, band_ref, pw_ref, ps_ref, u_ref):
    n = v_ref.shape[0]
    pos = lax.broadcasted_iota(I32, (n, LANES), 0)
    for g, w in enumerate(POOL_WINDOWS):
        gs = slice(g * POOL_GROUP_W, (g + 1) * POOL_GROUP_W)
        vt = v_ref[:, gs]
        s = _dot(band_ref[g], vt)
        u_ref[:, gs] = _pool_tail(s, _window_count(pos, w, n), vt, g, pw_ref, ps_ref)


def _pool_grid_kernel(v_ref, band_ref, pw_ref, ps_ref, u_ref, p_ref, *, rows):
    tile = 256
    n = rows * GRID_W
    pad = (max(POOL_WINDOWS) // 2) * GRID_W
    p_ref[0:pad, :] = jnp.zeros((pad, POOL_GROUP_W), F32)
    p_ref[pad + n:pad + n + pad, :] = jnp.zeros((pad, POOL_GROUP_W), F32)
    for g, w in enumerate(POOL_WINDOWS):
        gs = slice(g * POOL_GROUP_W, (g + 1) * POOL_GROUP_W)

        def colsum(t, carry):
            r0 = pl.multiple_of(t * tile, tile)
            p_ref[pl.ds(pad + r0, tile), :] = _dot(band_ref[g], v_ref[pl.ds(r0, tile), gs])
            return carry

        lax.fori_loop(0, n // tile, colsum, 0, unroll=4)

        def rowsum(t, carry):
            r0 = pl.multiple_of(t * tile, tile)
            s = p_ref[pl.ds(pad + r0 - (w // 2) * GRID_W, tile), :]
            for d in range(-(w // 2) + 1, w // 2):
                s = s + p_ref[pl.ds(pad + r0 + d * GRID_W, tile), :]
            tok = r0 + lax.broadcasted_iota(I32, (tile, LANES), 0)
            grid_row = jnp.right_shift(tok, GRID_W.bit_length() - 1)
            grid_col = jnp.bitwise_and(tok, GRID_W - 1)
            cnt = _window_count(grid_row, w, rows) * _window_count(grid_col, w, GRID_W)
            vt = v_ref[pl.ds(r0, tile), gs]
            u_ref[pl.ds(r0, tile), gs] = _pool_tail(s, cnt, vt, g, pw_ref, ps_ref)
            return carry

        lax.fori_loop(0, n // tile, rowsum, 0, unroll=2)


def _band_matrix(n, w):
    idx = np.arange(n)
    lo = np.clip(idx - w // 2, 0, n - 1)
    hi = np.clip(idx + w // 2 - 1, 0, n - 1)
    j = idx[None, :]
    return ((j >= lo[:, None]) & (j <= hi[:, None])).astype(np.float32)


def _pool_call(v, pw, ps, *, n_seq, seq_len, base_block, grid_rows):
    if grid_rows is None:
        bands = np.stack([_band_matrix(seq_len, w) for w in POOL_WINDOWS])
        kern = _pool_seq_kernel
        scratch = []
    else:
        per = 256 // GRID_W
        bands = np.stack([np.kron(np.eye(per, dtype=np.float32), _band_matrix(GRID_W, w)) for w in POOL_WINDOWS])
        kern = functools.partial(_pool_grid_kernel, rows=grid_rows)
        pad = (max(POOL_WINDOWS) // 2) * GRID_W
        scratch = [pltpu.VMEM((seq_len + 2 * pad, POOL_GROUP_W), F32)]
    bands = jnp.asarray(bands, BF16)
    const3 = lambda s: (0, 0, 0)
    return pl.pallas_call(
        kern,
        grid=(n_seq,),
        in_specs=[
            pl.BlockSpec((seq_len, D_MODEL), lambda s: (base_block + s, 0)),
            pl.BlockSpec(bands.shape, const3),
            pl.BlockSpec(pw.shape, const3),
            pl.BlockSpec(ps.shape, lambda s: (0, 0)),
        ],
        out_specs=pl.BlockSpec((seq_len, D_MODEL), lambda s: (s, 0)),
        out_shape=jax.ShapeDtypeStruct((n_seq * seq_len, D_MODEL), BF16),
        scratch_shapes=scratch,
        name="pool_mixer_len%d" % seq_len,
    )(v, bands, pw, ps)


def _route_kernel(yp_ref, ys_ref, up_ref, us_ref, xp_ref, xs_ref, mod_ref, wy_ref, wu_ref, nw_ref, wr_ref, rb_ref,
                  x1_ref, h2_ref, cls_ref, *, prompt_tiles):
    tm = x1_ref.shape[0]
    is_prompt = pl.program_id(0) < prompt_tiles
    y = jnp.where(is_prompt, yp_ref[...], ys_ref[...])
    u = jnp.where(is_prompt, up_ref[...], us_ref[...])
    x = jnp.where(is_prompt, xp_ref[...], xs_ref[...])
    mix = _dot(y, wy_ref[0]) + _dot(u, wu_ref[0])
    x1 = x + mod_ref[0, 2:3, :] * mix
    x1_ref[...] = x1
    h2 = (_rms(x1) * nw_ref[...]) * (1.0 + mod_ref[0, 4:5, :]) + mod_ref[0, 3:4, :]
    _store_slabs(h2_ref, h2)

    logits = _dot_split(h2, wr_ref[...]).T[0:N_EXPERTS, :]
    e = jnp.exp(logits - jnp.max(logits, axis=0, keepdims=True))
    probs = e / jnp.sum(e, axis=0, keepdims=True)
    sel = probs + rb_ref[...]
    s_rows = [sel[i:i + 1, :] for i in range(N_EXPERTS)]

    scores = []
    for g in range(N_EXPERT_GROUPS):
        a, b, c, d = s_rows[4 * g:4 * g + 4]
        hi1, lo1 = jnp.maximum(a, b), jnp.minimum(a, b)
        hi2, lo2 = jnp.maximum(c, d), jnp.minimum(c, d)
        scores.append(jnp.maximum(hi1, hi2) + jnp.maximum(jnp.minimum(hi1, hi2), jnp.maximum(lo1, lo2)))
    best_g = jnp.zeros((1, tm), I32)
    best_s = scores[0]
    for g in range(1, N_EXPERT_GROUPS):
        better = scores[g] > best_s
        best_g = jnp.where(better, g, best_g)
        best_s = jnp.where(better, scores[g], best_s)

    def in_group(rows, j):
        out = rows[j]
        for g in range(1, N_EXPERT_GROUPS):
            out = jnp.where(best_g == g, rows[4 * g + j], out)
        return out

    sg = [in_group(s_rows, j) for j in range(EXPERTS_PER_GROUP)]
    i1 = jnp.zeros((1, tm), I32)
    v1 = sg[0]
    for j in range(1, EXPERTS_PER_GROUP):
        better = sg[j] > v1
        i1 = jnp.where(better, j, i1)
        v1 = jnp.where(better, sg[j], v1)
    i2 = jnp.zeros((1, tm), I32)
    v2 = jnp.where(i1 == 0, NEG_INF, sg[0])
    for j in range(1, EXPERTS_PER_GROUP):
        cand = jnp.where(i1 == j, NEG_INF, sg[j])
        better = cand > v2
        i2 = jnp.where(better, j, i2)
        v2 = jnp.where(better, cand, v2)

    lo = jnp.minimum(i1, i2)
    hi = jnp.maximum(i1, i2)
    pair = jnp.where(lo == 0, hi - 1, jnp.where(lo == 1, hi + 1, 5))
    cls_ref[0] = best_g * len(EXPERT_PAIRS) + pair


def _route_call(yp, ys, up, us, xp, xs, mod, w_out, layer, nw, wr, rb, mod_row):
    t_p, t_s = xp.shape[0], xs.shape[0]
    t = t_p + t_s
    tm = TM_TOK
    row = lambda i: (i, 0)
    const = lambda i: (0, 0)
    return pl.pallas_call(
        functools.partial(_route_kernel, prompt_tiles=t_p // tm),
        grid=(t // tm,),
        in_specs=_group_specs(tm, D_SSD, t_p, t_s) + _group_specs(tm, D_MODEL, t_p, t_s)
        + _group_specs(tm, D_MODEL, t_p, t_s) + [
            pl.BlockSpec((1, 8, D_MODEL), lambda i: (mod_row(i * tm), 0, 0)),
            pl.BlockSpec((1, D_SSD, D_MODEL), lambda i: (layer, 0, 0)),
            pl.BlockSpec((1, D_MODEL, D_MODEL), lambda i: (layer, D_SSD // D_MODEL, 0)),
            pl.BlockSpec((1, D_MODEL), const),
            pl.BlockSpec(wr.shape, const),
            pl.BlockSpec(rb.shape, const),
        ],
        out_specs=[
            pl.BlockSpec((tm, D_MODEL), row),
            pl.BlockSpec((tm * SLAB, LANES), row),
            pl.BlockSpec((1, 1, tm), lambda i: (i, 0, 0)),
        ],
        out_shape=[
            jax.ShapeDtypeStruct((t, D_MODEL), F32),
            jax.ShapeDtypeStruct((t * SLAB, LANES), F32),
            jax.ShapeDtypeStruct((t // tm, 1, tm), I32),
        ],
        name="out_proj_router",
    )(yp, ys, up, us, xp, xs, mod, w_out, w_out, nw, wr, rb)


def _sort_kernel(cls_ref, pos_ref, tile_ref):
    cls = cls_ref[...]
    nr = cls.shape[0]
    ci = lax.broadcasted_iota(I32, (LANES, LANES), 0)
    cj = lax.broadcasted_iota(I32, (LANES, LANES), 1)
    upper = jnp.where(ci <= cj, 1.0, 0.0).astype(BF16)
    ri = lax.broadcasted_iota(I32, (nr, nr), 0)
    rj = lax.broadcasted_iota(I32, (nr, nr), 1)
    lower = jnp.where(rj < ri, 1.0, 0.0).astype(BF16)
    tile_idx = lax.broadcasted_iota(I32, (8, LANES), 1).astype(F32)
    pos = jnp.zeros(cls.shape, F32)
    tile_cls = jnp.zeros((8, LANES), F32)
    tiles_done = jnp.zeros((1, 1), F32)
    for k in range(N_CLASSES):
        member = cls == k
        mf = jnp.where(member, 1.0, 0.0)
        within = _dot(mf.astype(BF16), upper)
        row_tot = jnp.broadcast_to(within[:, LANES - 1:LANES], cls.shape)
        before = _dot(lower, row_tot.astype(BF16))
        rank = within - mf + before
        count = jnp.sum(jnp.sum(mf, axis=1, keepdims=True), axis=0, keepdims=True)
        n_tiles = jnp.floor((count + (TM_EXP - 1)) * (1.0 / TM_EXP))
        pos = pos + jnp.where(member, tiles_done * TM_EXP + rank, 0.0)
        tiles_done = tiles_done + n_tiles
        tile_cls = tile_cls + jnp.where(tile_idx >= tiles_done, 1.0, 0.0)
    pos_ref[...] = pos.astype(I32)
    r = lax.broadcasted_iota(I32, (8, LANES), 0)
    tile_ref[...] = jnp.where(r == 1, tiles_done, tile_cls).astype(I32)


def _sort_call(cls):
    t = cls.size
    rows = -(-t // (LANES * LANES)) * LANES
    cls2 = jnp.pad(cls.reshape(t // LANES, LANES), ((0, rows - t // LANES), (0, 0)), constant_values=N_CLASSES)
    pos, tiles = pl.pallas_call(
        _sort_kernel,
        out_shape=[jax.ShapeDtypeStruct(cls2.shape, I32), jax.ShapeDtypeStruct((8, LANES), I32)],
        name="route_sort",
    )(cls2)
    return pos.reshape(-1)[:t], tiles


def _slab(ref, tok):
    return ref.at[pl.ds(pl.multiple_of(tok * SLAB, SLAB), SLAB), :]


def _scatter_kernel(pos_ref, src_ref, dst_in, dst_ref, sem):
    del dst_in
    tm = src_ref.shape[0] // SLAB
    base = pl.program_id(0) * tm

    def row_copy(r):
        return pltpu.make_async_copy(_slab(src_ref, r), _slab(dst_ref, pos_ref[base + r]), sem)

    def body(i, carry):
        for p in range(DMA_THREADS):
            row_copy(i * DMA_THREADS + p).start(priority=p)
        return carry

    lax.fori_loop(0, tm // DMA_THREADS, body, 0, unroll=DMA_UNROLL // DMA_THREADS)

    def drain(r, carry):
        row_copy(r).wait()
        return carry

    lax.fori_loop(0, tm, drain, 0, unroll=DMA_UNROLL)


def _scatter_call(pos, src, n_tokens):
    tm = TM_TOK
    t = src.shape[0] // SLAB
    dst = jnp.zeros((n_tokens * SLAB, LANES), src.dtype)
    return pl.pallas_call(
        _scatter_kernel,
        grid_spec=pltpu.PrefetchScalarGridSpec(
            num_scalar_prefetch=1,
            grid=(t // tm,),
            in_specs=[pl.BlockSpec((tm * SLAB, LANES), lambda i, pos: (i, 0)),
                      pl.BlockSpec(memory_space=pl.ANY)],
            out_specs=pl.BlockSpec(memory_space=pl.ANY),
            scratch_shapes=[pltpu.SemaphoreType.DMA(())],
        ),
        out_shape=jax.ShapeDtypeStruct(dst.shape, dst.dtype),
        input_output_aliases={2: 0},
        name="route_scatter",
    )(pos, src, dst)


def _unpermute_kernel(pos_ref, x1_ref, mod_ref, nw_ref, y_hbm, op_ref, os_ref, buf, sem, *, final_norm,
                      prompt_tiles):
    tm = x1_ref.shape[0]
    base = pl.program_id(0) * tm

    def row_copy(r):
        return pltpu.make_async_copy(_slab(y_hbm, pos_ref[base + r]), _slab(buf, r), sem)

    def body(i, carry):
        for p in range(DMA_THREADS):
            row_copy(i * DMA_THREADS + p).start(priority=p)
        return carry

    lax.fori_loop(0, tm // DMA_THREADS, body, 0, unroll=DMA_UNROLL // DMA_THREADS)

    def drain(r, carry):
        row_copy(r).wait()
        return carry

    lax.fori_loop(0, tm, drain, 0, unroll=DMA_UNROLL)
    x2 = x1_ref[...] + mod_ref[0, 5:6, :] * _load_slabs(buf, tm)
    if final_norm:
        x2 = _rms(x2) * nw_ref[...]

    @pl.when(pl.program_id(0) < prompt_tiles)
    def _():
        op_ref[...] = x2

    @pl.when(pl.program_id(0) >= prompt_tiles)
    def _():
        os_ref[...] = x2


def _unpermute_call(pos, x1, mod, nw, y_sorted, mod_row, t_p, final_norm):
    t = x1.shape[0]
    tm = TM_TOK
    return pl.pallas_call(
        functools.partial(_unpermute_kernel, final_norm=final_norm, prompt_tiles=t_p // tm),
        grid_spec=pltpu.PrefetchScalarGridSpec(
            num_scalar_prefetch=1,
            grid=(t // tm,),
            in_specs=[pl.BlockSpec((tm, D_MODEL), lambda i, pos: (i, 0)),
                      pl.BlockSpec((1, 8, D_MODEL), lambda i, pos: (mod_row(i * tm), 0, 0)),
                      pl.BlockSpec((1, D_MODEL), lambda i, pos: (0, 0)),
                      pl.BlockSpec(memory_space=pl.ANY)],
            out_specs=_group_specs(tm, D_MODEL, t_p, t - t_p),
            scratch_shapes=[pltpu.VMEM((tm * SLAB, LANES), F32), pltpu.SemaphoreType.DMA(())],
        ),
        out_shape=[jax.ShapeDtypeStruct((t_p, D_MODEL), F32), jax.ShapeDtypeStruct((t - t_p, D_MODEL), F32)],
        name="moe_unpermute",
    )(pos, x1, mod, nw, y_sorted)


def _moe_kernel(ea_ref, eb_ref, nact_ref, x_ref, wr_ref, wga, wua, wda, wgb, wub, wdb, y_ref):
    j = pl.program_id(0)
    tm = x_ref.shape[0] // SLAB

    @pl.when(j < nact_ref[0])
    def _():
        x = _load_slabs(x_ref, tm).astype(BF16)
        logits = _dot(x, wr_ref[...])
        lane = lax.broadcasted_iota(I32, logits.shape, 1)
        l_a = jnp.sum(jnp.where(lane == ea_ref[j], logits, 0.0), axis=1, keepdims=True)
        l_b = jnp.sum(jnp.where(lane == eb_ref[j], logits, 0.0), axis=1, keepdims=True)
        gate_a = 1.0 / (1.0 + jnp.exp(l_b - l_a))
        gate_b = 1.0 / (1.0 + jnp.exp(l_a - l_b))

        def ffn(wg, wu, wd, gate):
            g = _dot(x, wg[0, 0].astype(BF16))
            u = _dot(x, wu[0, 0].astype(BF16))
            a = _silu(g) * u * gate
            return _dot(a.astype(BF16), wd[0, 0].astype(BF16))

        _store_slabs(y_ref, ffn(wga, wua, wda, gate_a) + ffn(wgb, wub, wdb, gate_b))

    @pl.when(j >= nact_ref[0])
    def _():
        y_ref[...] = jnp.zeros_like(y_ref)


def _moe_call(ea, eb, nact, xs, wr, w_gate, w_up, w_down, layer):
    n_rows = xs.shape[0] // SLAB
    tm = TM_EXP
    wa = lambda j, ea, eb, na: (layer, ea[j], 0, 0)
    wb = lambda j, ea, eb, na: (layer, eb[j], 0, 0)
    gu = (1, 1, D_MODEL, D_FF)
    dn = (1, 1, D_FF, D_MODEL)
    tile = pl.BlockSpec((tm * SLAB, LANES), lambda j, ea, eb, na: (j, 0))
    return pl.pallas_call(
        _moe_kernel,
        grid_spec=pltpu.PrefetchScalarGridSpec(
            num_scalar_prefetch=3,
            grid=(n_rows // tm,),
            in_specs=[tile, pl.BlockSpec(wr.shape, lambda j, ea, eb, na: (0, 0)),
                      pl.BlockSpec(gu, wa), pl.BlockSpec(gu, wa), pl.BlockSpec(dn, wa),
                      pl.BlockSpec(gu, wb), pl.BlockSpec(gu, wb), pl.BlockSpec(dn, wb)],
            out_specs=tile,
        ),
        out_shape=jax.ShapeDtypeStruct((n_rows * SLAB, LANES), F32),
        name="moe_ffn",
    )(ea, eb, nact, xs, wr, w_gate, w_up, w_down, w_gate, w_up, w_down)


def kernel(x_prompt, x_sample, state_ssm, c, c_ctx, w_ada, b_ada, norm1_w, w_in, conv_w, conv_b, dt_bias, a_log, d_skip, ssd_norm_w, pool_w, pool_scale, w_out, norm2_w, w_router, router_bias, w_gate, w_up, w_down, final_norm_w):
    n_p, len_p, d = x_prompt.shape
    n_s, len_s, _ = x_sample.shape
    depth = w_ada.shape[0]
    t_p, t_s = n_p * len_p, n_s * len_s
    t = t_p + t_s
    assert d == D_MODEL and len_p % TM_TOK == 0 and len_s % TM_PROJ == 0 and t_p % TM_PROJ == 0
    assert len_s % GRID_W == 0 and t_p % len_s == 0

    def mod_row(tok):
        return jnp.where(tok < t_p, 0, 1 + (tok - t_p) // len_s)

    xp, xs = x_prompt.reshape(t_p, d), x_sample.reshape(t_s, d)

    cvec = jnp.zeros((8, d), F32).at[0].set(c_ctx).at[1:1 + n_s].set(c)
    ada = _ada_call(cvec, w_ada, b_ada)
    mods = jnp.pad(ada[:, :1 + n_s].reshape(depth, 1 + n_s, 6, d), ((0, 0), (0, 0), (0, 2), (0, 0)))

    e_np = np.zeros((LANES, D_SSD), np.float32)
    e_np[N_HEADS:2 * N_HEADS] = np.kron(np.eye(N_HEADS, dtype=np.float32), np.ones((1, HEADDIM), np.float32))
    e_mat = jnp.asarray(e_np, BF16)
    wr = jnp.pad(w_router, ((0, 0), (0, LANES - N_EXPERTS)))
    rb = jnp.broadcast_to(router_bias[:, None], (N_EXPERTS, TM_TOK))
    pairs = np.asarray(EXPERT_PAIRS, np.int32)
    n_sorted = t + N_CLASSES * TM_EXP
    n_tiles = n_sorted // TM_EXP
    assert n_tiles <= LANES

    init_all = state_ssm.reshape(n_s, depth, 2, D_SSD, D_STATE)
    w_out_b = w_out.astype(BF16)

    new_states = []
    for l in range(depth):
        w_l = w_in[l]
        wz = w_l[:, :D_SSD].astype(BF16)
        wx = w_l[:, D_SSD:D_SSD + CONV_DIM].astype(BF16)
        o = D_SSD + CONV_DIM
        wd = jnp.pad(w_l[:, o:o + 2 * N_HEADS], ((0, 0), (0, LANES - 2 * N_HEADS))).astype(BF16)
        wv = w_l[:, o + 2 * N_HEADS:].astype(BF16)
        z, xbc, v, dt = _inproj_call(xp, xs, mods[l], norm1_w[l][None], wz, wx, wv, wd, mod_row)

        dtb = jnp.broadcast_to(dt_bias[l].reshape(2 * N_HEADS, 1), (2 * N_HEADS, CHUNK))
        alog = jnp.broadcast_to(a_log[l].reshape(2 * N_HEADS, 1), (2 * N_HEADS, CHUNK))
        dsk = jnp.repeat(d_skip[l], HEADDIM, axis=1)
        ra, rw, cc = _decay_call(dt, dtb, alog)
        ssd_args = (conv_w[l], conv_b[l][None], dsk, ssd_norm_w[l][None], e_mat)
        y_p, fin = _ssd_call(xbc, ra, rw, cc, z, None, l, *ssd_args, n_seq=n_p, nc=len_p // CHUNK, base=0,
                             has_final=True)
        y_s, _ = _ssd_call(xbc, ra, rw, cc, z, init_all, l, *ssd_args, n_seq=n_s, nc=len_s // CHUNK,
                           base=t_p // CHUNK, has_final=False)
        new_states.append(fin)

        pw = pool_w[l].astype(BF16)
        ps = pool_scale[l][None]
        u_p = _pool_call(v, pw, ps, n_seq=n_p, seq_len=len_p, base_block=0, grid_rows=None)
        u_s = _pool_call(v, pw, ps, n_seq=n_s, seq_len=len_s, base_block=t_p // len_s, grid_rows=len_s // GRID_W)

        x1, h2, cls = _route_call(y_p, y_s, u_p, u_s, xp, xs, mods[l], w_out_b, l, norm2_w[l][None], wr, rb, mod_row)

        pos, tiles = _sort_call(cls)
        nact = tiles[1, 0:1]
        tcls = tiles[0, :n_tiles]
        tcls = jnp.minimum(tcls, tcls[jnp.maximum(nact[0] - 1, 0)])
        tcls = jnp.minimum(tcls, N_CLASSES - 1)
        grp = tcls // len(EXPERT_PAIRS)
        pr = tcls % len(EXPERT_PAIRS)
        ea = grp * EXPERTS_PER_GROUP + jnp.asarray(pairs[:, 0])[pr]
        eb = grp * EXPERTS_PER_GROUP + jnp.asarray(pairs[:, 1])[pr]

        h2s = _scatter_call(pos, h2, n_sorted)
        y_sorted = _moe_call(ea, eb, nact, h2s, wr.astype(BF16), w_gate, w_up, w_down, l)
        last = l == depth - 1
        xp, xs = _unpermute_call(pos, x1, mods[l], final_norm_w[None], y_sorted, mod_row, t_p, last)

    y_prompt = xp.reshape(n_p, len_p, d)
    y_sample = xs.reshape(n_s, len_s, d)
    ns = jnp.stack(new_states, axis=1)
    new_state_ssm = ns.reshape(n_p, depth, 2, N_HEADS, HEADDIM, D_STATE)
    return (y_prompt, y_sample, new_state_ssm)
```

```python
import functools

import numpy as np
import jax
import jax.numpy as jnp
from jax import lax
from jax.experimental import pallas as pl
from jax.experimental.pallas import tpu as pltpu

F32 = jnp.float32
BF16 = jnp.bfloat16
I32 = jnp.int32

D_MODEL = 1024
HEADDIM = 64
N_HEADS = 16
N_BC_GROUPS = 2
HEADS_PER_BC = N_HEADS // N_BC_GROUPS
D_STATE = 128
CHUNK = 128
D_SSD = N_HEADS * HEADDIM
D_BC = N_BC_GROUPS * D_STATE
CONV_DIM = D_SSD + 2 * D_BC
POOL_WINDOWS = (2, 4, 8, 16)
POOL_GROUP_W = 256
GRID_W = 64
N_EXPERTS = 16
EXPERTS_PER_GROUP = 4
N_EXPERT_GROUPS = N_EXPERTS // EXPERTS_PER_GROUP
D_FF = 512
EPS = 1e-6

LANES = 128
HEAD_QUAD = 4
QUAD_W = HEAD_QUAD * HEADDIM
EXPERT_PAIRS = ((0, 1), (0, 2), (1, 2), (1, 3), (0, 3), (2, 3))
N_CLASSES = N_EXPERT_GROUPS * len(EXPERT_PAIRS)

TM_PROJ = 512
TM_TOK = 256
TM_EXP = 256
SLAB = D_MODEL // LANES
DMA_UNROLL = 8
DMA_THREADS = 2
NEG_INF = float("-inf")


def _dot(a, b):
    return jnp.dot(a, b, preferred_element_type=F32)


def _split_bf16(x):
    hi = x.astype(BF16)
    lo = (x - hi.astype(F32)).astype(BF16)
    return hi, lo


def _dot_split(a, w):
    ah, al = _split_bf16(a)
    wh, wl = _split_bf16(w)
    return _dot(ah, wh) + (_dot(al, wh) + _dot(ah, wl))


def _silu(x):
    return x / (1.0 + jnp.exp(-x))


def _rms(x):
    return x * lax.rsqrt(jnp.mean(x * x, axis=-1, keepdims=True) + EPS)


def _store_slabs(ref, val):
    n = val.shape[0]
    for k in range(SLAB):
        ref[pl.ds(k, n, stride=SLAB), :] = val[:, k * LANES:(k + 1) * LANES]


def _load_slabs(ref, n):
    return jnp.concatenate([ref[pl.ds(k, n, stride=SLAB), :] for k in range(SLAB)], axis=1)


def _ada_kernel(c_ref, w_ref, b_ref, o_ref):
    o_ref[0] = _dot_split(_silu(c_ref[...]), w_ref[0]) + b_ref[0]


def _ada_call(cvec, w_ada, b_ada):
    depth, d, n = w_ada.shape
    tn = 1024
    return pl.pallas_call(
        _ada_kernel,
        grid=(depth, n // tn),
        in_specs=[
            pl.BlockSpec((8, d), lambda l, j: (0, 0)),
            pl.BlockSpec((1, d, tn), lambda l, j: (l, 0, j)),
            pl.BlockSpec((1, 1, tn), lambda l, j: (l, 0, j)),
        ],
        out_specs=pl.BlockSpec((1, 8, tn), lambda l, j: (l, 0, j)),
        out_shape=jax.ShapeDtypeStruct((depth, 8, n), F32),
        name="ada_ln",
    )(cvec, w_ada, b_ada.reshape(depth, 1, n))


def _group_specs(tm, width, p_rows, s_rows):
    pt, st = p_rows // tm, s_rows // tm
    return [pl.BlockSpec((tm, width), lambda i, *_: (jnp.minimum(i, pt - 1), 0)),
            pl.BlockSpec((tm, width), lambda i, *_: (jnp.clip(i - pt, 0, st - 1), 0))]


def _inproj_kernel(xp_ref, xs_ref, mod_ref, nw_ref, wz_ref, wx_ref, wv_ref, wd_ref,
                   z_ref, xbc_ref, v_ref, dt_ref, *, prompt_tiles):
    x = jnp.where(pl.program_id(0) < prompt_tiles, xp_ref[...], xs_ref[...])
    h = _rms(x) * nw_ref[...]
    h = h * (1.0 + mod_ref[0, 1:2, :]) + mod_ref[0, 0:1, :]
    hb = h.astype(BF16)
    z_ref[...] = _dot(hb, wz_ref[...]).astype(BF16)
    xbc_ref[...] = _dot(hb, wx_ref[...])
    v_ref[...] = _dot(hb, wv_ref[...]).astype(BF16)
    dt_ref[...] = _dot(hb, wd_ref[...])


def _inproj_call(xp, xs, mod, nw, wz, wx, wv, wd, mod_row):
    t = xp.shape[0] + xs.shape[0]
    tm = TM_PROJ
    const = lambda i: (0, 0)
    row = lambda i: (i, 0)
    return pl.pallas_call(
        functools.partial(_inproj_kernel, prompt_tiles=xp.shape[0] // tm),
        grid=(t // tm,),
        in_specs=_group_specs(tm, D_MODEL, xp.shape[0], xs.shape[0]) + [
            pl.BlockSpec((1, 8, D_MODEL), lambda i: (mod_row(i * tm), 0, 0)),
            pl.BlockSpec((1, D_MODEL), const),
            pl.BlockSpec(wz.shape, const),
            pl.BlockSpec(wx.shape, const),
            pl.BlockSpec(wv.shape, const),
            pl.BlockSpec(wd.shape, const),
        ],
        out_specs=[
            pl.BlockSpec((tm, D_SSD), row),
            pl.BlockSpec((tm, CONV_DIM), row),
            pl.BlockSpec((tm, D_MODEL), row),
            pl.BlockSpec((tm, LANES), row),
        ],
        out_shape=[
            jax.ShapeDtypeStruct((t, D_SSD), BF16),
            jax.ShapeDtypeStruct((t, CONV_DIM), F32),
            jax.ShapeDtypeStruct((t, D_MODEL), BF16),
            jax.ShapeDtypeStruct((t, LANES), F32),
        ],
        name="in_proj",
    )(xp, xs, mod, nw, wz, wx, wv, wd)


LOG2E = 1.4426950408889634
PREP_CHUNKS = 8


def _decay_kernel(dt_ref, dtb_ref, alog_ref, ra_ref, rw_ref, cc_ref):
    q = CHUNK
    ci = lax.broadcasted_iota(I32, (q, q), 0)
    cj = lax.broadcasted_iota(I32, (q, q), 1)
    upper = jnp.where(ci <= cj, 1.0, 0.0).astype(BF16)
    t_parts = [dt_ref[k * q:(k + 1) * q, :].T for k in range(PREP_CHUNKS)]
    for d in range(2):
        x = jnp.concatenate([t[d * N_HEADS:(d + 1) * N_HEADS, :] for t in t_parts], axis=0)
        tile = lambda r: jnp.concatenate([r[d * N_HEADS:(d + 1) * N_HEADS, :]] * PREP_CHUNKS, axis=0)
        xdt = x + tile(dtb_ref)
        dtv = jnp.maximum(xdt, 0.0) + jnp.log(1.0 + jnp.exp(-jnp.abs(xdt)))
        a_step = dtv * (-jnp.exp(tile(alog_ref)))
        hi = a_step.astype(BF16)
        r1 = a_step - hi.astype(F32)
        mid = r1.astype(BF16)
        lo = (r1 - mid.astype(F32)).astype(BF16)
        pref = _dot(hi, upper) + (_dot(mid, upper) + _dot(lo, upper))
        total = pref[:, q - 1:q]
        cum = pref if d == 0 else total - pref + a_step
        ra_ref[d] = (cum - jnp.log(dtv)) * LOG2E
        rw_ref[d] = dtv * jnp.exp(total - cum)
        cum2 = cum * LOG2E
        e_cum = jnp.exp(cum)
        pad = jnp.zeros((q - 2 * N_HEADS, q), F32)
        for k in range(PREP_CHUNKS):
            rs = slice(k * N_HEADS, (k + 1) * N_HEADS)
            cc_ref[d, k * q:(k + 1) * q, :] = jnp.concatenate([cum2[rs], e_cum[rs], pad], axis=0).T


def _decay_call(dt, dtb, alog):
    t = dt.shape[0]
    q = CHUNK
    tm = PREP_CHUNKS * q
    rows = PREP_CHUNKS * N_HEADS
    n_rows = t // q * N_HEADS
    const = lambda i: (0, 0)
    return pl.pallas_call(
        _decay_kernel,
        grid=(t // tm,),
        in_specs=[pl.BlockSpec((tm, LANES), lambda i: (i, 0)),
                  pl.BlockSpec(dtb.shape, const),
                  pl.BlockSpec(alog.shape, const)],
        out_specs=[pl.BlockSpec((2, rows, q), lambda i: (0, i, 0)),
                   pl.BlockSpec((2, rows, q), lambda i: (0, i, 0)),
                   pl.BlockSpec((2, tm, LANES), lambda i: (0, i, 0))],
        out_shape=[jax.ShapeDtypeStruct((2, n_rows, q), F32),
                   jax.ShapeDtypeStruct((2, n_rows, q), F32),
                   jax.ShapeDtypeStruct((2, t, LANES), F32)],
        name="ssd_decay_tables",
    )(dt, dtb, alog)


EXP2_CLAMP = 100.0


def _ssd_kernel(*refs, nc, has_init, has_final):
    it = iter(refs)
    xm_ref, xp_ref, xn_ref, ra_ref, rw_ref, cc_ref, z_ref = (next(it) for _ in range(7))
    init_ref = next(it) if has_init else None
    cw_ref, cb_ref, dsk_ref, nw_ref, e_ref = (next(it) for _ in range(5))
    y_ref = next(it)
    fin_ref = next(it) if has_final else None
    st_ref, yb_ref, xs_c, c_c, bt_c, g_c = (next(it) for _ in range(6))

    ph = pl.program_id(1)
    c = pl.program_id(2)
    fwd = ph == 1
    cc = jnp.where(fwd, c, nc - 1 - c)
    q = CHUNK
    r0 = pl.multiple_of(cc * q, q)
    r0g = pl.multiple_of(cc * (N_BC_GROUPS * q), N_BC_GROUPS * q)

    @pl.when(c == 0)
    def _():
        if has_init:
            st_ref[...] = init_ref[0, 0, 0].T
        else:
            st_ref[...] = jnp.zeros_like(st_ref)

    @pl.when(ph == 0)
    def _():
        xm = xm_ref[...]
        prev_row = jnp.where(cc > 0, xp_ref[7:8, :], 0.0)
        next0 = jnp.where(cc < nc - 1, xn_ref[0:1, :], 0.0)
        next1 = jnp.where(cc < nc - 1, xn_ref[1:2, :], 0.0)
        rows = lax.broadcasted_iota(I32, (q, 1), 0)
        x_m1 = jnp.where(rows == 0, prev_row, pltpu.roll(xm, 1, 0))
        x_p1 = jnp.where(rows == q - 1, next0, pltpu.roll(xm, q - 1, 0))
        x_p2 = jnp.where(rows == q - 2, next0, jnp.where(rows == q - 1, next1, pltpu.roll(xm, q - 2, 0)))
        conv = (cw_ref[0:1, :] * x_m1 + cw_ref[1:2, :] * xm + cw_ref[2:3, :] * x_p1
                + cw_ref[3:4, :] * x_p2 + cb_ref[...])
        act = _silu(conv)
        xs_c[pl.ds(r0, q), :] = act[:, :D_SSD].astype(BF16)
        c_c[pl.ds(r0, q), :] = act[:, D_SSD + D_BC:].astype(BF16)
        for g in range(N_BC_GROUPS):
            b_g = act[:, D_SSD + g * D_STATE:D_SSD + (g + 1) * D_STATE]
            c_g = act[:, D_SSD + D_BC + g * D_STATE:D_SSD + D_BC + (g + 1) * D_STATE].astype(BF16)
            bt_c[pl.ds(r0g + g * q, q), :] = b_g.T.astype(BF16)
            g_c[pl.ds(r0g + g * q, q), :] = lax.dot_general(
                c_g, b_g.astype(BF16), (((1,), (1,)), ((), ())), preferred_element_type=F32).astype(BF16)

    xs_b = xs_c[pl.ds(r0, q), :]
    c_all = c_c[pl.ds(r0, q), :]
    c_parts = [c_all[:, g * D_STATE:(g + 1) * D_STATE] for g in range(N_BC_GROUPS)]
    bt_parts = [bt_c[pl.ds(r0g + g * q, q), :] for g in range(N_BC_GROUPS)]

    li = lax.broadcasted_iota(I32, (q, q), 0)
    si = lax.broadcasted_iota(I32, (q, q), 1)
    dls = li - si
    causal = jnp.where(fwd, dls, -dls) >= 0
    g_parts = [jnp.where(causal, g_c[pl.ds(r0g + g * q, q), :], jnp.zeros((q, q), BF16))
               for g in range(N_BC_GROUPS)]

    src_exp = ra_ref[0]
    w_t = rw_ref[0].astype(BF16)
    col = cc_ref[0]

    lane_q = lax.broadcasted_iota(I32, (q, QUAD_W), 1) // HEADDIM
    y_parts, s_parts = [], []
    for qd in range(N_HEADS // HEAD_QUAD):
        top, bot = [], []
        for j in range(HEAD_QUAD):
            h = qd * HEAD_QUAD + j
            g = h // HEADS_PER_BC
            seg = jnp.minimum(col[:, h:h + 1] - src_exp[h:h + 1, :], EXP2_CLAMP)
            top.append(jnp.exp2(seg).astype(BF16) * g_parts[g])
            bot.append(bt_parts[g] * w_t[h:h + 1, :])
        lhs = jnp.concatenate([jnp.concatenate(top, axis=1), jnp.concatenate(bot, axis=1)], axis=0)
        xq = xs_b[:, qd * QUAD_W:(qd + 1) * QUAD_W]
        rhs = jnp.concatenate([jnp.where(lane_q == j, xq, jnp.zeros_like(xq)) for j in range(HEAD_QUAD)], axis=0)
        res = _dot(lhs, rhs)
        y_parts.append(res[:q])
        s_parts.append(res[q:])
    y_diag = jnp.concatenate(y_parts, axis=1)
    st_new = jnp.concatenate(s_parts, axis=1)

    st = st_ref[...]
    st_b = st.astype(BF16)
    half = D_SSD // N_BC_GROUPS
    y_off = jnp.concatenate([_dot(c_parts[g], st_b[:, g * half:(g + 1) * half]) for g in range(N_BC_GROUPS)],
                            axis=1)
    col_hi, col_lo = _split_bf16(col)
    e_cum = _dot(col_hi, e_ref[...]) + _dot(col_lo, e_ref[...])
    y_dir = y_diag + y_off * e_cum
    e_tot = jnp.where(fwd, e_cum[q - 1:q, :], e_cum[0:1, :])
    st_next = st * e_tot + st_new
    st_ref[...] = st_next

    if has_final:
        @pl.when(c == nc - 1)
        def _():
            fin_ref[0, 0] = st_next.T

    @pl.when(ph == 0)
    def _():
        yb_ref[pl.ds(r0, q), :] = y_dir.astype(BF16)

    @pl.when(ph == 1)
    def _():
        skip = xs_b.astype(F32) * (dsk_ref[0:1, :] + dsk_ref[1:2, :])
        y = y_dir + yb_ref[pl.ds(r0, q), :].astype(F32) + skip
        y = y * _silu(z_ref[...].astype(F32))
        y_ref[...] = (_rms(y) * nw_ref[...]).astype(BF16)


def _ssd_call(xbc, ra, rw, cc, z, init, layer, cw, cb, dsk, nw, e_mat, *, n_seq, nc, base, has_final):
    t = xbc.shape[0]
    q = CHUNK
    has_init = init is not None
    last8 = t // 8 - 1

    def chunk_of(s, ph, c):
        return base + s * nc + jnp.where(ph == 1, c, nc - 1 - c)

    def conv_chunk(s, ph, c):
        return base + s * nc + (1 - ph) * (nc - 1 - c)

    main = lambda s, ph, c: (conv_chunk(s, ph, c), 0)
    prev = lambda s, ph, c: (jnp.maximum(conv_chunk(s, ph, c) * (q // 8) - 1, 0), 0)
    nxt = lambda s, ph, c: (jnp.minimum((conv_chunk(s, ph, c) + 1) * (q // 8), last8), 0)
    table = lambda s, ph, c: (1 - ph, chunk_of(s, ph, c), 0)
    const = lambda s, ph, c: (0, 0)

    in_specs = [
        pl.BlockSpec((q, CONV_DIM), main),
        pl.BlockSpec((8, CONV_DIM), prev),
        pl.BlockSpec((8, CONV_DIM), nxt),
        pl.BlockSpec((1, N_HEADS, q), table),
        pl.BlockSpec((1, N_HEADS, q), table),
        pl.BlockSpec((1, q, LANES), table),
        pl.BlockSpec((q, D_SSD), lambda s, ph, c: (base + s * nc + ph * c, 0)),
    ]
    args = [xbc, xbc, xbc, ra, rw, cc, z]
    if has_init:
        in_specs.append(pl.BlockSpec((1, 1, 1, D_SSD, D_STATE), lambda s, ph, c: (s, layer, 1 - ph, 0, 0)))
        args.append(init)
    for a in (cw, cb, dsk, nw, e_mat):
        in_specs.append(pl.BlockSpec(a.shape, const))
        args.append(a)

    out_specs = [pl.BlockSpec((q, D_SSD), lambda s, ph, c: (s * nc + ph * c, 0))]
    out_shape = [jax.ShapeDtypeStruct((n_seq * nc * q, D_SSD), BF16)]
    if has_final:
        out_specs.append(pl.BlockSpec((1, 1, D_SSD, D_STATE), lambda s, ph, c: (s, 1 - ph, 0, 0)))
        out_shape.append(jax.ShapeDtypeStruct((n_seq, 2, D_SSD, D_STATE), F32))

    out = pl.pallas_call(
        functools.partial(_ssd_kernel, nc=nc, has_init=has_init, has_final=has_final),
        grid=(n_seq, 2, nc),
        in_specs=in_specs,
        out_specs=out_specs,
        out_shape=out_shape,
        scratch_shapes=[
            pltpu.VMEM((D_STATE, D_SSD), F32),
            pltpu.VMEM((nc * q, D_SSD), BF16),
            pltpu.VMEM((nc * q, D_SSD), BF16),
            pltpu.VMEM((nc * q, D_BC), BF16),
            pltpu.VMEM((nc * N_BC_GROUPS * q, q), BF16),
            pltpu.VMEM((nc * N_BC_GROUPS * q, q), BF16),
        ],
        compiler_params=pltpu.CompilerParams(dimension_semantics=("arbitrary", "arbitrary", "arbitrary")),
        name="ssd_mixer_nc%d" % nc,
    )(*args)
    return out if has_final else (out[0], None)


def _window_count(idx, w, n):
    hi = jnp.minimum(idx + (w // 2 - 1), n - 1)
    lo = jnp.maximum(idx - w // 2, 0)
    return hi - lo + 1


def _pool_tail(s, cnt, vt, g, pw_ref, ps_ref):
    cnt2 = jnp.concatenate([cnt, cnt], axis=1).astype(F32)
    upre = s / cnt2 - vt.astype(F32)
    gs = slice(g * POOL_GROUP_W, (g + 1) * POOL_GROUP_W)
    return (_dot(upre.astype(BF16), pw_ref[g]) * ps_ref[:, gs]).astype(BF16)


def _pool_seq_kernel(v_ref, band_ref, pw_ref, ps_ref, u_ref):
    n = v_ref.shape[0]
    pos = lax.broadcasted_iota(I32, (n, LANES), 0)
    for g, w in enumerate(POOL_WINDOWS):
        gs = slice(g * POOL_GROUP_W, (g + 1) * POOL_GROUP_W)
        vt = v_ref[:, gs]
        s = _dot(band_ref[g], vt)
        u_ref[:, gs] = _pool_tail(s, _window_count(pos, w, n), vt, g, pw_ref, ps_ref)


def _pool_grid_kernel(v_ref, band_ref, pw_ref, ps_ref, u_ref, p_ref, *, rows):
    tile = 256
    n = rows * GRID_W
    pad = (max(POOL_WINDOWS) // 2) * GRID_W
    p_ref[0:pad, :] = jnp.zeros((pad, POOL_GROUP_W), F32)
    p_ref[pad + n:pad + n + pad, :] = jnp.zeros((pad, POOL_GROUP_W), F32)
    for g, w in enumerate(POOL_WINDOWS):
        gs = slice(g * POOL_GROUP_W, (g + 1) * POOL_GROUP_W)

        def colsum(t, carry):
            r0 = pl.multiple_of(t * tile, tile)
            p_ref[pl.ds(pad + r0, tile), :] = _dot(band_ref[g], v_ref[pl.ds(r0, tile), gs])
            return carry

        lax.fori_loop(0, n // tile, colsum, 0, unroll=4)

        def rowsum(t, carry):
            r0 = pl.multiple_of(t * tile, tile)
            s = p_ref[pl.ds(pad + r0 - (w // 2) * GRID_W, tile), :]
            for d in range(-(w // 2) + 1, w // 2):
                s = s + p_ref[pl.ds(pad + r0 + d * GRID_W, tile), :]
            tok = r0 + lax.broadcasted_iota(I32, (tile, LANES), 0)
            grid_row = jnp.right_shift(tok, GRID_W.bit_length() - 1)
            grid_col = jnp.bitwise_and(tok, GRID_W - 1)
            cnt = _window_count(grid_row, w, rows) * _window_count(grid_col, w, GRID_W)
            vt = v_ref[pl.ds(r0, tile), gs]
            u_ref[pl.ds(r0, tile), gs] = _pool_tail(s, cnt, vt, g, pw_ref, ps_ref)
            return carry

        lax.fori_loop(0, n // tile, rowsum, 0, unroll=2)


def _band_matrix(n, w):
    idx = np.arange(n)
    lo = np.clip(idx - w // 2, 0, n - 1)
    hi = np.clip(idx + w // 2 - 1, 0, n - 1)
    j = idx[None, :]
    return ((j >= lo[:, None]) & (j <= hi[:, None])).astype(np.float32)


def _pool_call(v, pw, ps, *, n_seq, seq_len, base_block, grid_rows):
    if grid_rows is None:
        bands = np.stack([_band_matrix(seq_len, w) for w in POOL_WINDOWS])
        kern = _pool_seq_kernel
        scratch = []
    else:
        per = 256 // GRID_W
        bands = np.stack([np.kron(np.eye(per, dtype=np.float32), _band_matrix(GRID_W, w)) for w in POOL_WINDOWS])
        kern = functools.partial(_pool_grid_kernel, rows=grid_rows)
        pad = (max(POOL_WINDOWS) // 2) * GRID_W
        scratch = [pltpu.VMEM((seq_len + 2 * pad, POOL_GROUP_W), F32)]
    bands = jnp.asarray(bands, BF16)
    const3 = lambda s: (0, 0, 0)
    return pl.pallas_call(
        kern,
        grid=(n_seq,),
        in_specs=[
            pl.BlockSpec((seq_len, D_MODEL), lambda s: (base_block + s, 0)),
            pl.BlockSpec(bands.shape, const3),
            pl.BlockSpec(pw.shape, const3),
            pl.BlockSpec(ps.shape, lambda s: (0, 0)),
        ],
        out_specs=pl.BlockSpec((seq_len, D_MODEL), lambda s: (s, 0)),
        out_shape=jax.ShapeDtypeStruct((n_seq * seq_len, D_MODEL), BF16),
        scratch_shapes=scratch,
        name="pool_mixer_len%d" % seq_len,
    )(v, bands, pw, ps)


def _route_kernel(yp_ref, ys_ref, up_ref, us_ref, xp_ref, xs_ref, mod_ref, wy_ref, wu_ref, nw_ref, wr_ref, rb_ref,
                  x1_ref, h2_ref, cls_ref, *, prompt_tiles):
    tm = x1_ref.shape[0]
    is_prompt = pl.program_id(0) < prompt_tiles
    y = jnp.where(is_prompt, yp_ref[...], ys_ref[...])
    u = jnp.where(is_prompt, up_ref[...], us_ref[...])
    x = jnp.where(is_prompt, xp_ref[...], xs_ref[...])
    mix = _dot(y, wy_ref[0]) + _dot(u, wu_ref[0])
    x1 = x + mod_ref[0, 2:3, :] * mix
    x1_ref[...] = x1
    h2 = (_rms(x1) * nw_ref[...]) * (1.0 + mod_ref[0, 4:5, :]) + mod_ref[0, 3:4, :]
    _store_slabs(h2_ref, h2)

    logits = _dot_split(h2, wr_ref[...]).T[0:N_EXPERTS, :]
    e = jnp.exp(logits - jnp.max(logits, axis=0, keepdims=True))
    probs = e / jnp.sum(e, axis=0, keepdims=True)
    sel = probs + rb_ref[...]
    s_rows = [sel[i:i + 1, :] for i in range(N_EXPERTS)]

    scores = []
    for g in range(N_EXPERT_GROUPS):
        a, b, c, d = s_rows[4 * g:4 * g + 4]
        hi1, lo1 = jnp.maximum(a, b), jnp.minimum(a, b)
        hi2, lo2 = jnp.maximum(c, d), jnp.minimum(c, d)
        scores.append(jnp.maximum(hi1, hi2) + jnp.maximum(jnp.minimum(hi1, hi2), jnp.maximum(lo1, lo2)))
    best_g = jnp.zeros((1, tm), I32)
    best_s = scores[0]
    for g in range(1, N_EXPERT_GROUPS):
        better = scores[g] > best_s
        best_g = jnp.where(better, g, best_g)
        best_s = jnp.where(better, scores[g], best_s)

    def in_group(rows, j):
        out = rows[j]
        for g in range(1, N_EXPERT_GROUPS):
            out = jnp.where(best_g == g, rows[4 * g + j], out)
        return out

    sg = [in_group(s_rows, j) for j in range(EXPERTS_PER_GROUP)]
    i1 = jnp.zeros((1, tm), I32)
    v1 = sg[0]
    for j in range(1, EXPERTS_PER_GROUP):
        better = sg[j] > v1
        i1 = jnp.where(better, j, i1)
        v1 = jnp.where(better, sg[j], v1)
    i2 = jnp.zeros((1, tm), I32)
    v2 = jnp.where(i1 == 0, NEG_INF, sg[0])
    for j in range(1, EXPERTS_PER_GROUP):
        cand = jnp.where(i1 == j, NEG_INF, sg[j])
        better = cand > v2
        i2 = jnp.where(better, j, i2)
        v2 = jnp.where(better, cand, v2)

    lo = jnp.minimum(i1, i2)
    hi = jnp.maximum(i1, i2)
    pair = jnp.zeros_like(lo)
    for idx, (p_lo, p_hi) in enumerate(EXPERT_PAIRS):
        pair = jnp.where((lo == p_lo) & (hi == p_hi), idx, pair)
    cls_ref[0] = best_g * len(EXPERT_PAIRS) + pair


def _route_call(yp, ys, up, us, xp, xs, mod, w_out, layer, nw, wr, rb, mod_row):
    t_p, t_s = xp.shape[0], xs.shape[0]
    t = t_p + t_s
    tm = TM_TOK
    row = lambda i: (i, 0)
    const = lambda i: (0, 0)
    return pl.pallas_call(
        functools.partial(_route_kernel, prompt_tiles=t_p // tm),
        grid=(t // tm,),
        in_specs=_group_specs(tm, D_SSD, t_p, t_s) + _group_specs(tm, D_MODEL, t_p, t_s)
        + _group_specs(tm, D_MODEL, t_p, t_s) + [
            pl.BlockSpec((1, 8, D_MODEL), lambda i: (mod_row(i * tm), 0, 0)),
            pl.BlockSpec((1, D_SSD, D_MODEL), lambda i: (layer, 0, 0)),
            pl.BlockSpec((1, D_MODEL, D_MODEL), lambda i: (layer, D_SSD // D_MODEL, 0)),
            pl.BlockSpec((1, D_MODEL), const),
            pl.BlockSpec(wr.shape, const),
            pl.BlockSpec(rb.shape, const),
        ],
        out_specs=[
            pl.BlockSpec((tm, D_MODEL), row),
            pl.BlockSpec((tm * SLAB, LANES), row),
            pl.BlockSpec((1, 1, tm), lambda i: (i, 0, 0)),
        ],
        out_shape=[
            jax.ShapeDtypeStruct((t, D_MODEL), F32),
            jax.ShapeDtypeStruct((t * SLAB, LANES), F32),
            jax.ShapeDtypeStruct((t // tm, 1, tm), I32),
        ],
        name="out_proj_router",
    )(yp, ys, up, us, xp, xs, mod, w_out, w_out, nw, wr, rb)


def _sort_kernel(cls_ref, pos_ref, tile_ref):
    cls = cls_ref[...]
    nr = cls.shape[0]
    ci = lax.broadcasted_iota(I32, (LANES, LANES), 0)
    cj = lax.broadcasted_iota(I32, (LANES, LANES), 1)
    upper = jnp.where(ci <= cj, 1.0, 0.0).astype(BF16)
    ri = lax.broadcasted_iota(I32, (nr, nr), 0)
    rj = lax.broadcasted_iota(I32, (nr, nr), 1)
    lower = jnp.where(rj < ri, 1.0, 0.0).astype(BF16)
    tile_idx = lax.broadcasted_iota(I32, (8, LANES), 1).astype(F32)
    pos = jnp.zeros(cls.shape, F32)
    tile_cls = jnp.zeros((8, LANES), F32)
    tiles_done = jnp.zeros((1, 1), F32)
    for k in range(N_CLASSES):
        member = cls == k
        mf = jnp.where(member, 1.0, 0.0)
        within = _dot(mf.astype(BF16), upper)
        row_tot = jnp.broadcast_to(within[:, LANES - 1:LANES], cls.shape)
        before = _dot(lower, row_tot.astype(BF16))
        rank = within - mf + before
        count = jnp.sum(jnp.sum(mf, axis=1, keepdims=True), axis=0, keepdims=True)
        n_tiles = jnp.floor((count + (TM_EXP - 1)) * (1.0 / TM_EXP))
        pos = pos + jnp.where(member, tiles_done * TM_EXP + rank, 0.0)
        tiles_done = tiles_done + n_tiles
        tile_cls = tile_cls + jnp.where(tile_idx >= tiles_done, 1.0, 0.0)
    pos_ref[...] = pos.astype(I32)
    r = lax.broadcasted_iota(I32, (8, LANES), 0)
    tile_ref[...] = jnp.where(r == 1, tiles_done, tile_cls).astype(I32)


def _sort_call(cls):
    t = cls.size
    rows = -(-t // (LANES * LANES)) * LANES
    cls2 = jnp.pad(cls.reshape(t // LANES, LANES), ((0, rows - t // LANES), (0, 0)), constant_values=N_CLASSES)
    pos, tiles = pl.pallas_call(
        _sort_kernel,
        out_shape=[jax.ShapeDtypeStruct(cls2.shape, I32), jax.ShapeDtypeStruct((8, LANES), I32)],
        name="route_sort",
    )(cls2)
    return pos.reshape(-1)[:t], tiles


def _slab(ref, tok):
    return ref.at[pl.ds(pl.multiple_of(tok * SLAB, SLAB), SLAB), :]


def _scatter_kernel(pos_ref, src_ref, dst_in, dst_ref, sem):
    del dst_in
    tm = src_ref.shape[0] // SLAB
    base = pl.program_id(0) * tm

    def row_copy(r):
        return pltpu.make_async_copy(_slab(src_ref, r), _slab(dst_ref, pos_ref[base + r]), sem)

    def body(i, carry):
        for p in range(DMA_THREADS):
            row_copy(i * DMA_THREADS + p).start(priority=p)
        return carry

    lax.fori_loop(0, tm // DMA_THREADS, body, 0, unroll=DMA_UNROLL // DMA_THREADS)

    def drain(r, carry):
        row_copy(r).wait()
        return carry

    lax.fori_loop(0, tm, drain, 0, unroll=DMA_UNROLL)


def _scatter_call(pos, src, n_tokens):
    tm = TM_TOK
    t = src.shape[0] // SLAB
    dst = jnp.zeros((n_tokens * SLAB, LANES), src.dtype)
    return pl.pallas_call(
        _scatter_kernel,
        grid_spec=pltpu.PrefetchScalarGridSpec(
            num_scalar_prefetch=1,
            grid=(t // tm,),
            in_specs=[pl.BlockSpec((tm * SLAB, LANES), lambda i, pos: (i, 0)),
                      pl.BlockSpec(memory_space=pl.ANY)],
            out_specs=pl.BlockSpec(memory_space=pl.ANY),
            scratch_shapes=[pltpu.SemaphoreType.DMA(())],
        ),
        out_shape=jax.ShapeDtypeStruct(dst.shape, dst.dtype),
        input_output_aliases={2: 0},
        name="route_scatter",
    )(pos, src, dst)


def _unpermute_kernel(pos_ref, x1_ref, mod_ref, nw_ref, y_hbm, op_ref, os_ref, buf, sem, *, final_norm,
                      prompt_tiles):
    tm = x1_ref.shape[0]
    base = pl.program_id(0) * tm

    def row_copy(r):
        return pltpu.make_async_copy(_slab(y_hbm, pos_ref[base + r]), _slab(buf, r), sem)

    def body(i, carry):
        for p in range(DMA_THREADS):
            row_copy(i * DMA_THREADS + p).start(priority=p)
        return carry

    lax.fori_loop(0, tm // DMA_THREADS, body, 0, unroll=DMA_UNROLL // DMA_THREADS)

    def drain(r, carry):
        row_copy(r).wait()
        return carry

    lax.fori_loop(0, tm, drain, 0, unroll=DMA_UNROLL)
    x2 = x1_ref[...] + mod_ref[0, 5:6, :] * _load_slabs(buf, tm)
    if final_norm:
        x2 = _rms(x2) * nw_ref[...]

    @pl.when(pl.program_id(0) < prompt_tiles)
    def _():
        op_ref[...] = x2

    @pl.when(pl.program_id(0) >= prompt_tiles)
    def _():
        os_ref[...] = x2


def _unpermute_call(pos, x1, mod, nw, y_sorted, mod_row, t_p, final_norm):
    t = x1.shape[0]
    tm = TM_TOK
    return pl.pallas_call(
        functools.partial(_unpermute_kernel, final_norm=final_norm, prompt_tiles=t_p // tm),
        grid_spec=pltpu.PrefetchScalarGridSpec(
            num_scalar_prefetch=1,
            grid=(t // tm,),
            in_specs=[pl.BlockSpec((tm, D_MODEL), lambda i, pos: (i, 0)),
                      pl.BlockSpec((1, 8, D_MODEL), lambda i, pos: (mod_row(i * tm), 0, 0)),
                      pl.BlockSpec((1, D_MODEL), lambda i, pos: (0, 0)),
                      pl.BlockSpec(memory_space=pl.ANY)],
            out_specs=_group_specs(tm, D_MODEL, t_p, t - t_p),
            scratch_shapes=[pltpu.VMEM((tm * SLAB, LANES), F32), pltpu.SemaphoreType.DMA(())],
        ),
        out_shape=[jax.ShapeDtypeStruct((t_p, D_MODEL), F32), jax.ShapeDtypeStruct((t - t_p, D_MODEL), F32)],
        name="moe_unpermute",
    )(pos, x1, mod, nw, y_sorted)


def _moe_kernel(ea_ref, eb_ref, nact_ref, x_ref, wr_ref, wga, wua, wda, wgb, wub, wdb, y_ref):
    j = pl.program_id(0)
    tm = x_ref.shape[0] // SLAB

    @pl.when(j < nact_ref[0])
    def _():
        x = _load_slabs(x_ref, tm).astype(BF16)
        logits = _dot(x, wr_ref[...])
        lane = lax.broadcasted_iota(I32, logits.shape, 1)
        l_a = jnp.sum(jnp.where(lane == ea_ref[j], logits, 0.0), axis=1, keepdims=True)
        l_b = jnp.sum(jnp.where(lane == eb_ref[j], logits, 0.0), axis=1, keepdims=True)
        gate_a = 1.0 / (1.0 + jnp.exp(l_b - l_a))
        gate_b = 1.0 / (1.0 + jnp.exp(l_a - l_b))

        def ffn(wg, wu, wd, gate):
            g = _dot(x, wg[0, 0].astype(BF16))
            u = _dot(x, wu[0, 0].astype(BF16))
            a = _silu(g) * u * gate
            return _dot(a.astype(BF16), wd[0, 0].astype(BF16))

        _store_slabs(y_ref, ffn(wga, wua, wda, gate_a) + ffn(wgb, wub, wdb, gate_b))

    @pl.when(j >= nact_ref[0])
    def _():
        y_ref[...] = jnp.zeros_like(y_ref)


def _moe_call(ea, eb, nact, xs, wr, w_gate, w_up, w_down, layer):
    n_rows = xs.shape[0] // SLAB
    tm = TM_EXP
    wa = lambda j, ea, eb, na: (layer, ea[j], 0, 0)
    wb = lambda j, ea, eb, na: (layer, eb[j], 0, 0)
    gu = (1, 1, D_MODEL, D_FF)
    dn = (1, 1, D_FF, D_MODEL)
    tile = pl.BlockSpec((tm * SLAB, LANES), lambda j, ea, eb, na: (j, 0))
    return pl.pallas_call(
        _moe_kernel,
        grid_spec=pltpu.PrefetchScalarGridSpec(
            num_scalar_prefetch=3,
            grid=(n_rows // tm,),
            in_specs=[tile, pl.BlockSpec(wr.shape, lambda j, ea, eb, na: (0, 0)),
                      pl.BlockSpec(gu, wa), pl.BlockSpec(gu, wa), pl.BlockSpec(dn, wa),
                      pl.BlockSpec(gu, wb), pl.BlockSpec(gu, wb), pl.BlockSpec(dn, wb)],
            out_specs=tile,
        ),
        out_shape=jax.ShapeDtypeStruct((n_rows * SLAB, LANES), F32),
        name="moe_ffn",
    )(ea, eb, nact, xs, wr, w_gate, w_up, w_down, w_gate, w_up, w_down)


def kernel(x_prompt, x_sample, state_ssm, c, c_ctx, w_ada, b_ada, norm1_w, w_in, conv_w, conv_b, dt_bias, a_log, d_skip, ssd_norm_w, pool_w, pool_scale, w_out, norm2_w, w_router, router_bias, w_gate, w_up, w_down, final_norm_w):
    n_p, len_p, d = x_prompt.shape
    n_s, len_s, _ = x_sample.shape
    depth = w_ada.shape[0]
    t_p, t_s = n_p * len_p, n_s * len_s
    t = t_p + t_s
    assert d == D_MODEL and len_p % TM_TOK == 0 and len_s % TM_PROJ == 0 and t_p % TM_PROJ == 0
    assert len_s % GRID_W == 0 and t_p % len_s == 0

    def mod_row(tok):
        return jnp.where(tok < t_p, 0, 1 + (tok - t_p) // len_s)

    xp, xs = x_prompt.reshape(t_p, d), x_sample.reshape(t_s, d)

    cvec = jnp.zeros((8, d), F32).at[0].set(c_ctx).at[1:1 + n_s].set(c)
    ada = _ada_call(cvec, w_ada, b_ada)
    mods = jnp.pad(ada[:, :1 + n_s].reshape(depth, 1 + n_s, 6, d), ((0, 0), (0, 0), (0, 2), (0, 0)))

    e_np = np.zeros((LANES, D_SSD), np.float32)
    e_np[N_HEADS:2 * N_HEADS] = np.kron(np.eye(N_HEADS, dtype=np.float32), np.ones((1, HEADDIM), np.float32))
    e_mat = jnp.asarray(e_np, BF16)
    wr = jnp.pad(w_router, ((0, 0), (0, LANES - N_EXPERTS)))
    rb = jnp.broadcast_to(router_bias[:, None], (N_EXPERTS, TM_TOK))
    pairs = np.asarray(EXPERT_PAIRS, np.int32)
    n_sorted = t + N_CLASSES * TM_EXP
    n_tiles = n_sorted // TM_EXP
    assert n_tiles <= LANES

    init_all = state_ssm.reshape(n_s, depth, 2, D_SSD, D_STATE)
    w_out_b = w_out.astype(BF16)

    new_states = []
    for l in range(depth):
        w_l = w_in[l]
        wz = w_l[:, :D_SSD].astype(BF16)
        wx = w_l[:, D_SSD:D_SSD + CONV_DIM].astype(BF16)
        o = D_SSD + CONV_DIM
        wd = jnp.pad(w_l[:, o:o + 2 * N_HEADS], ((0, 0), (0, LANES - 2 * N_HEADS))).astype(BF16)
        wv = w_l[:, o + 2 * N_HEADS:].astype(BF16)
        z, xbc, v, dt = _inproj_call(xp, xs, mods[l], norm1_w[l][None], wz, wx, wv, wd, mod_row)

        dtb = jnp.broadcast_to(dt_bias[l].reshape(2 * N_HEADS, 1), (2 * N_HEADS, CHUNK))
        alog = jnp.broadcast_to(a_log[l].reshape(2 * N_HEADS, 1), (2 * N_HEADS, CHUNK))
        dsk = jnp.repeat(d_skip[l], HEADDIM, axis=1)
        ra, rw, cc = _decay_call(dt, dtb, alog)
        ssd_args = (conv_w[l], conv_b[l][None], dsk, ssd_norm_w[l][None], e_mat)
        y_p, fin = _ssd_call(xbc, ra, rw, cc, z, None, l, *ssd_args, n_seq=n_p, nc=len_p // CHUNK, base=0,
                             has_final=True)
        y_s, _ = _ssd_call(xbc, ra, rw, cc, z, init_all, l, *ssd_args, n_seq=n_s, nc=len_s // CHUNK,
                           base=t_p // CHUNK, has_final=False)
        new_states.append(fin)

        pw = pool_w[l].astype(BF16)
        ps = pool_scale[l][None]
        u_p = _pool_call(v, pw, ps, n_seq=n_p, seq_len=len_p, base_block=0, grid_rows=None)
        u_s = _pool_call(v, pw, ps, n_seq=n_s, seq_len=len_s, base_block=t_p // len_s, grid_rows=len_s // GRID_W)

        x1, h2, cls = _route_call(y_p, y_s, u_p, u_s, xp, xs, mods[l], w_out_b, l, norm2_w[l][None], wr, rb, mod_row)

        pos, tiles = _sort_call(cls)
        nact = tiles[1, 0:1]
        tcls = tiles[0, :n_tiles]
        tcls = jnp.minimum(tcls, tcls[jnp.maximum(nact[0] - 1, 0)])
        tcls = jnp.minimum(tcls, N_CLASSES - 1)
        grp = tcls // len(EXPERT_PAIRS)
        pr = tcls % len(EXPERT_PAIRS)
        ea = grp * EXPERTS_PER_GROUP + jnp.asarray(pairs[:, 0])[pr]
        eb = grp * EXPERTS_PER_GROUP + jnp.asarray(pairs[:, 1])[pr]

        h2s = _scatter_call(pos, h2, n_sorted)
        y_sorted = _moe_call(ea, eb, nact, h2s, wr.astype(BF16), w_gate, w_up, w_down, l)
        last = l == depth - 1
        xp, xs = _unpermute_call(pos, x1, mods[l], final_norm_w[None], y_sorted, mod_row, t_p, last)

    y_prompt = xp.reshape(n_p, len_p, d)
    y_sample = xs.reshape(n_s, len_s, d)
    ns = jnp.stack(new_states, axis=1)
    new_state_ssm = ns.reshape(n_p, depth, 2, N_HEADS, HEADDIM, D_STATE)
    return (y_prompt, y_sample, new_state_ssm)
```
